```python
import jax, jax.numpy as jnp
from jax import lax
import numpy as np

D_MODEL = 1024
BATCH = 4
SEQ = 4096
DEPTH = 1

D_MIX = D_MODEL
MLA_HEADS = 4
QK_NOPE_DIM = 128
QK_ROPE_DIM = 64
QK_HEAD_DIM = QK_NOPE_DIM + QK_ROPE_DIM
V_HEAD_DIM = 128
Q_LORA_RANK = 256
KV_LORA_RANK = 128
ROPE_THETA = 10000.0
ATTN_Q_BLOCK = 128
MLA_WIDTH = MLA_HEADS * V_HEAD_DIM
SG_HEADS = 4
SG_HEAD_DIM = 128
SG_WIDTH = SG_HEADS * SG_HEAD_DIM
SG_CHUNK = 128
IN_COLS = Q_LORA_RANK + KV_LORA_RANK + QK_ROPE_DIM + 2 * SG_WIDTH
PEER_HEADS = 8
PEER_N_KEYS = 128
PEER_N_EXPERTS = PEER_N_KEYS * PEER_N_KEYS
PEER_TOPK = 16
PEER_QUERY_DIM = 256
PEER_HALF_DIM = PEER_QUERY_DIM // 2
PEER_TOKEN_BLOCK = 128
N_MOD = 6
EPS = 1e-6

kernel_name = "hymba_mla_gmlp_peer_adaln_encoder"


def rmsnorm(x, g):
    xf = x.astype(jnp.float32)
    y = xf * lax.rsqrt(jnp.mean(xf * xf, axis=-1, keepdims=True) + EPS)
    return (y * g.astype(jnp.float32)).astype(x.dtype)


def modulate(h, shift, scale):
    return h * (1.0 + scale[:, None, :]) + shift[:, None, :]


def rope(x, cos, sin):
    half = x.shape[-1] // 2
    x1, x2 = x[..., :half], x[..., half:]
    return jnp.concatenate([x1 * cos - x2 * sin, x2 * cos + x1 * sin], axis=-1)


def rope_tables(positions, dtype):
    inv_freq = 1.0 / (ROPE_THETA ** (jnp.arange(0, QK_ROPE_DIM, 2, dtype=jnp.float32) / QK_ROPE_DIM))
    ang = positions.astype(jnp.float32)[..., None] * inv_freq
    return jnp.cos(ang).astype(dtype), jnp.sin(ang).astype(dtype)


def mla_group(c_q, c_kv, k_rope_raw, cos, sin, g_q_a, w_uq, g_kv_a, w_ukv):
    B, S, _ = c_q.shape
    q = (rmsnorm(c_q, g_q_a) @ w_uq).reshape(B, S, MLA_HEADS, QK_HEAD_DIM)
    q_rope = rope(q[..., QK_NOPE_DIM:], cos[:, :, None, :], sin[:, :, None, :])
    q = jnp.concatenate([q[..., :QK_NOPE_DIM], q_rope], axis=-1)
    kv = (rmsnorm(c_kv, g_kv_a) @ w_ukv).reshape(B, S, MLA_HEADS, QK_NOPE_DIM + V_HEAD_DIM)
    k_nope, v = kv[..., :QK_NOPE_DIM], kv[..., QK_NOPE_DIM:]
    k_rope = rope(k_rope_raw, cos, sin)
    k = jnp.concatenate(
        [k_nope, jnp.broadcast_to(k_rope[:, :, None, :], (B, S, MLA_HEADS, QK_ROPE_DIM))], axis=-1)
    scale = QK_HEAD_DIM ** -0.5
    nb = S // ATTN_Q_BLOCK
    qb = q.reshape(B, nb, ATTN_Q_BLOCK, MLA_HEADS, QK_HEAD_DIM).transpose(1, 0, 2, 3, 4)

    def attend(q_blk):
        s = jnp.einsum('bqhd,bkhd->bhqk', q_blk, k).astype(jnp.float32) * scale
        p = jax.nn.softmax(s, axis=-1).astype(v.dtype)
        return jnp.einsum('bhqk,bkhd->bqhd', p, v)

    o = lax.map(attend, qb)
    return o.transpose(1, 0, 2, 3, 4).reshape(B, S, MLA_WIDTH)


def sgu_group(u, v, g_sg, w_sg, b_sg):
    B, S, _ = u.shape
    u = jax.nn.gelu(u, approximate=False)
    v = rmsnorm(jax.nn.gelu(v, approximate=False), g_sg)
    nc = S // SG_CHUNK
    v = v.reshape(B, nc, SG_CHUNK, SG_HEADS, SG_HEAD_DIM)
    mix = jnp.einsum('hpq,bnqhd->bnphd', w_sg, v) + b_sg.T[None, None, :, :, None]
    out = u.reshape(B, nc, SG_CHUNK, SG_HEADS, SG_HEAD_DIM) * mix
    return out.reshape(B, S, SG_WIDTH)


def peer(h, w_q, keys, U, V):
    B, S, D = h.shape
    T = B * S
    ht = h.reshape(T, D)
    q = (ht @ w_q).reshape(T, PEER_HEADS, 2, PEER_HALF_DIM)
    s1 = jnp.einsum('thd,hkd->thk', q[:, :, 0], keys[:, 0]).astype(jnp.float32)
    s2 = jnp.einsum('thd,hkd->thk', q[:, :, 1], keys[:, 1]).astype(jnp.float32)
    v1, i1 = lax.top_k(s1, PEER_TOPK)
    v2, i2 = lax.top_k(s2, PEER_TOPK)
    cand = (v1[..., :, None] + v2[..., None, :]).reshape(T, PEER_HEADS, PEER_TOPK * PEER_TOPK)
    sc, ci = lax.top_k(cand, PEER_TOPK)
    idx = (jnp.take_along_axis(i1, ci // PEER_TOPK, axis=-1) * PEER_N_KEYS
           + jnp.take_along_axis(i2, ci % PEER_TOPK, axis=-1))
    gates = jax.nn.softmax(sc, axis=-1).astype(h.dtype)

    nblk = T // PEER_TOKEN_BLOCK
    hb = ht.reshape(nblk, PEER_TOKEN_BLOCK, D)
    ib = idx.reshape(nblk, PEER_TOKEN_BLOCK, PEER_HEADS, PEER_TOPK)
    gb = gates.reshape(nblk, PEER_TOKEN_BLOCK, PEER_HEADS, PEER_TOPK)

    def experts(args):
        xb, ids, g = args
        u_sel = U[ids]
        a = jnp.einsum('chkd,cd->chk', u_sel, xb)
        w = jax.nn.gelu(a, approximate=False) * g
        return jnp.einsum('chk,chkd->cd', w, V[ids])

    out = lax.map(experts, (hb, ib, gb))
    return out.reshape(B, S, D)


def setup_inputs(seed: int = 0) -> dict:
    key = jax.random.key(seed)
    ks = jax.random.split(key, 26)
    nrm = lambda k, shape, s: jax.random.normal(k, shape, jnp.float32) * s
    gain = lambda k, shape: 1.0 + 0.02 * jax.random.normal(k, shape, jnp.float32)
    L = DEPTH
    positions = (jnp.arange(SEQ, dtype=jnp.int32)[None, :]
                 + jax.random.randint(ks[2], (BATCH, 1), 0, 1024, dtype=jnp.int32))
    return {
        "x": nrm(ks[0], (BATCH, SEQ, D_MODEL), 1.0),
        "c": nrm(ks[1], (BATCH, D_MODEL), 1.0),
        "positions": positions,
        "w_ada": nrm(ks[3], (L, D_MODEL, N_MOD * D_MODEL), 0.5 * D_MODEL ** -0.5),
        "b_ada": nrm(ks[4], (L, N_MOD * D_MODEL), 0.02),
        "g_norm1": gain(ks[5], (L, D_MODEL)),
        "w_in": nrm(ks[6], (L, D_MODEL, IN_COLS), D_MODEL ** -0.5),
        "g_q_a": gain(ks[7], (L, Q_LORA_RANK)),
        "w_uq": nrm(ks[8], (L, Q_LORA_RANK, MLA_HEADS * QK_HEAD_DIM), Q_LORA_RANK ** -0.5),
        "g_kv_a": gain(ks[9], (L, KV_LORA_RANK)),
        "w_ukv": nrm(ks[10], (L, KV_LORA_RANK, MLA_HEADS * (QK_NOPE_DIM + V_HEAD_DIM)), KV_LORA_RANK ** -0.5),
        "g_sg": gain(ks[11], (L, SG_WIDTH)),
        "w_sg": nrm(ks[12], (L, SG_HEADS, SG_CHUNK, SG_CHUNK), SG_CHUNK ** -0.5),
        "b_sg": 1.0 + nrm(ks[13], (L, SG_HEADS, SG_CHUNK), 0.1),
        "g_attn_out": gain(ks[14], (L, MLA_WIDTH)),
        "g_sg_out": gain(ks[15], (L, SG_WIDTH)),
        "w_o": nrm(ks[16], (L, D_MIX, D_MODEL), D_MIX ** -0.5),
        "g_norm2": gain(ks[17], (L, D_MODEL)),
        "w_peer_q": nrm(ks[18], (L, D_MODEL, PEER_HEADS * PEER_QUERY_DIM), D_MODEL ** -0.5),
        "peer_keys": nrm(ks[19], (L, PEER_HEADS, 2, PEER_N_KEYS, PEER_HALF_DIM), PEER_HALF_DIM ** -0.5),
        "peer_u": nrm(ks[20], (L, PEER_N_EXPERTS, D_MODEL), D_MODEL ** -0.5),
        "peer_v": nrm(ks[21], (L, PEER_N_EXPERTS, D_MODEL), PEER_HEADS ** -0.5),
        "g_final": gain(ks[22], (D_MODEL,)),
    }


def reference(x, c, positions, w_ada, b_ada, g_norm1, w_in, g_q_a, w_uq, g_kv_a, w_ukv,
              g_sg, w_sg, b_sg, g_attn_out, g_sg_out, w_o, g_norm2, w_peer_q, peer_keys,
              peer_u, peer_v, g_final):
    cos, sin = rope_tables(positions, x.dtype)
    c_act = jax.nn.silu(c)
    o1 = Q_LORA_RANK
    o2 = o1 + KV_LORA_RANK
    o3 = o2 + QK_ROPE_DIM
    o4 = o3 + SG_WIDTH
    for l in range(DEPTH):
        mod = c_act @ w_ada[l] + b_ada[l]
        shift1, scale1, gate1, shift2, scale2, gate2 = jnp.split(mod, N_MOD, axis=-1)
        h = modulate(rmsnorm(x, g_norm1[l]), shift1, scale1)
        z = h @ w_in[l]
        y_attn = mla_group(z[..., :o1], z[..., o1:o2], z[..., o2:o3], cos, sin,
                           g_q_a[l], w_uq[l], g_kv_a[l], w_ukv[l])
        y_sg = sgu_group(z[..., o3:o4], z[..., o4:], g_sg[l], w_sg[l], b_sg[l])
        y = jnp.concatenate([rmsnorm(y_attn, g_attn_out[l]), rmsnorm(y_sg, g_sg_out[l])], axis=-1)
        x = x + gate1[:, None, :] * (y @ w_o[l])
        h2 = modulate(rmsnorm(x, g_norm2[l]), shift2, scale2)
        x = x + gate2[:, None, :] * peer(h2, w_peer_q[l], peer_keys[l], peer_u[l], peer_v[l])
    return rmsnorm(x, g_final)
```

```python
import functools
import math

import jax
import jax.numpy as jnp
from jax import lax
from jax.experimental import pallas as pl
from jax.experimental.pallas import tpu as pltpu

LANES = 128
SUBLANES = 8
VMEM_LIMIT_BYTES = 56 * 1024 * 1024

EPS = 1e-6
ROPE_THETA = 10000.0
HEADS = 4
NOPE = 128
ROPE = 64
VDIM = 128
QK_DIM = NOPE + ROPE
QK_PAD = 2 * LANES
Q_RANK = 256
KV_RANK = 128
SG_HEADS = 4
SG_DIM = 128
SG_CHUNK = 128
SG_WIDTH = SG_HEADS * SG_DIM
PEER_HEADS = 8
N_KEYS = 128
TOPK = 16
HALF_DIM = 128
N_SEL = PEER_HEADS * TOPK

G_PITCH = N_KEYS + SUBLANES

_BF = jnp.bfloat16
_F32 = jnp.float32


def _cparams(sem):
    return pltpu.CompilerParams(dimension_semantics=sem, vmem_limit_bytes=VMEM_LIMIT_BYTES)


def _rms(x, g):
    return x * lax.rsqrt(jnp.mean(x * x, axis=-1, keepdims=True) + EPS) * g


def _gelu(x):
    return 0.5 * x * (1.0 + lax.erf(x * (1.0 / math.sqrt(2.0))))


def _dot(a, b):
    return jnp.dot(a, b, preferred_element_type=_F32)


def _dot_nt(a, b):
    return lax.dot_general(a, b, (((1,), (1,)), ((), ())), preferred_element_type=_F32)


def _adaln_kernel(c_ref, w_ref, b_ref, o_ref):
    c = c_ref[...]
    c_act = (c * jax.nn.sigmoid(c)).astype(_BF)
    o_ref[...] = _dot(c_act, w_ref[...].astype(_BF)) + b_ref[...]


def _adaln(c_pad, w_ada, b_ada):
    rows, d = c_pad.shape
    n = w_ada.shape[1]
    tn = 1024
    return pl.pallas_call(
        _adaln_kernel,
        grid=(n // tn,),
        in_specs=[
            pl.BlockSpec((rows, d), lambda j: (0, 0)),
            pl.BlockSpec((d, tn), lambda j: (0, j)),
            pl.BlockSpec((1, tn), lambda j: (0, j)),
        ],
        out_specs=pl.BlockSpec((rows, tn), lambda j: (0, j)),
        out_shape=jax.ShapeDtypeStruct((rows, n), _F32),
        compiler_params=_cparams(("arbitrary",)),
        name="adaln",
    )(c_pad, w_ada, b_ada)


def _inproj_kernel(x_ref, pos_ref, mod_ref, g1_ref, win_ref, gq_ref, wuq_ref, gkv_ref, wukv_ref,
                   invf_ref, gsg_ref, wsg_ref, bsg_ref, gsgo_ref,
                   q_ref, k_ref, v_ref, ysg_ref, *, scale):
    ts = x_ref.shape[1]
    x = x_ref[0]
    shift1 = mod_ref[0, 0:1, :]
    scale1 = mod_ref[0, 1:2, :]
    h = _rms(x, g1_ref[...]) * (1.0 + scale1) + shift1
    z = _dot(h.astype(_BF), win_ref[...])

    o_kv = Q_RANK
    o_kr = o_kv + KV_RANK
    o_krot = o_kr + LANES
    o_u = o_krot + LANES
    o_v = o_u + SG_WIDTH

    ang = pos_ref[0].astype(_F32) * invf_ref[...]
    cos2 = jnp.cos(ang)
    sin2 = jnp.sin(ang)

    cq = _rms(z[:, :Q_RANK], gq_ref[...]).astype(_BF)
    qall = _dot(cq, wuq_ref[...])
    for hd in range(HEADS):
        q_nope = qall[:, hd * NOPE:(hd + 1) * NOPE]
        q_r = qall[:, HEADS * NOPE + hd * LANES: HEADS * NOPE + (hd + 1) * LANES]
        q_rot = qall[:, HEADS * (NOPE + LANES) + hd * LANES: HEADS * (NOPE + LANES) + (hd + 1) * LANES]
        q_ref[0, hd, :, 0:LANES] = (q_nope * scale).astype(_BF)
        q_ref[0, hd, :, LANES:QK_PAD] = ((q_r * cos2 + q_rot * sin2) * scale).astype(_BF)

    ckv = _rms(z[:, o_kv:o_kr], gkv_ref[...]).astype(_BF)
    kv = _dot(ckv, wukv_ref[...])
    k_rope = (z[:, o_kr:o_krot] * cos2 + z[:, o_krot:o_u] * sin2).astype(_BF)
    for hd in range(HEADS):
        base = hd * (NOPE + VDIM)
        k_ref[0, hd, :, 0:LANES] = kv[:, base:base + NOPE].astype(_BF)
        k_ref[0, hd, :, LANES:QK_PAD] = k_rope
        v_ref[0, hd] = kv[:, base + NOPE:base + NOPE + VDIM].astype(_BF)

    u = _gelu(z[:, o_u:o_v])
    vg = _rms(_gelu(z[:, o_v:o_v + SG_WIDTH]), gsg_ref[...]).astype(_BF)
    for ch in range(ts // SG_CHUNK):
        r0 = ch * SG_CHUNK
        parts = []
        for hd in range(SG_HEADS):
            c0 = hd * SG_DIM
            mix = _dot(wsg_ref[hd], vg[r0:r0 + SG_CHUNK, c0:c0 + SG_DIM]) + bsg_ref[hd]
            parts.append(u[r0:r0 + SG_CHUNK, c0:c0 + SG_DIM] * mix)
        y = jnp.concatenate(parts, axis=-1)
        ysg_ref[0, r0:r0 + SG_CHUNK, :] = _rms(y, gsgo_ref[...]).astype(_BF)


def _inproj(x, pos3, mod3, g1, win_ext, gq, wuq_ext, gkv, wukv, invf, gsg, wsg, bsg3, gsgo, ts):
    b, s, d = x.shape
    nz = win_ext.shape[1]
    full = lambda *shape: pl.BlockSpec(shape, lambda bi, i: (0,) * len(shape))
    return pl.pallas_call(
        functools.partial(_inproj_kernel, scale=QK_DIM ** -0.5),
        grid=(b, s // ts),
        in_specs=[
            pl.BlockSpec((1, ts, d), lambda bi, i: (bi, i, 0)),
            pl.BlockSpec((1, ts, 1), lambda bi, i: (bi, i, 0)),
            pl.BlockSpec((1, 6, d), lambda bi, i: (bi, 0, 0)),
            full(1, d),
            full(d, nz),
            full(1, Q_RANK),
            full(Q_RANK, wuq_ext.shape[1]),
            full(1, KV_RANK),
            full(KV_RANK, wukv.shape[1]),
            full(1, LANES),
            full(1, SG_WIDTH),
            full(SG_HEADS, SG_CHUNK, SG_CHUNK),
            full(SG_HEADS, SG_CHUNK, 1),
            full(1, SG_WIDTH),
        ],
        out_specs=[
            pl.BlockSpec((1, HEADS, ts, QK_PAD), lambda bi, i: (bi, 0, i, 0)),
            pl.BlockSpec((1, HEADS, ts, QK_PAD), lambda bi, i: (bi, 0, i, 0)),
            pl.BlockSpec((1, HEADS, ts, VDIM), lambda bi, i: (bi, 0, i, 0)),
            pl.BlockSpec((1, ts, SG_WIDTH), lambda bi, i: (bi, i, 0)),
        ],
        out_shape=[
            jax.ShapeDtypeStruct((b, HEADS, s, QK_PAD), _BF),
            jax.ShapeDtypeStruct((b, HEADS, s, QK_PAD), _BF),
            jax.ShapeDtypeStruct((b, HEADS, s, VDIM), _BF),
            jax.ShapeDtypeStruct((b, s, SG_WIDTH), _BF),
        ],
        compiler_params=_cparams(("arbitrary", "arbitrary")),
        name="inproj",
    )(x, pos3, mod3, g1, win_ext, gq, wuq_ext, gkv, wukv, invf, gsg, wsg, bsg3, gsgo)


def _attn_kernel(q_ref, k_ref, v_ref, o_ref, *, tk):
    s = k_ref.shape[2]
    tq = q_ref.shape[2]
    q = q_ref[0, 0]

    def body(j, carry):
        m, l, acc = carry
        k0 = pl.multiple_of(j * tk, tk)
        kc = k_ref[0, 0, pl.ds(k0, tk), :]
        vc = v_ref[0, 0, pl.ds(k0, tk), :]
        sc = _dot_nt(q, kc)
        m_new = jnp.maximum(m, jnp.max(sc, axis=-1, keepdims=True))
        alpha = jnp.exp(m - m_new)
        p = jnp.exp(sc - m_new)
        l_new = alpha * l + jnp.sum(p, axis=-1, keepdims=True)
        acc_new = alpha * acc + _dot(p.astype(_BF), vc)
        return m_new, l_new, acc_new

    m0 = jnp.full((tq, 1), -jnp.inf, _F32)
    l0 = jnp.zeros((tq, 1), _F32)
    a0 = jnp.zeros((tq, VDIM), _F32)
    _, l, acc = lax.fori_loop(0, s // tk, body, (m0, l0, a0))
    o_ref[0] = (acc / l).astype(_BF)


def _attention(q, k, v, tq, tk):
    b, h, s, _ = q.shape
    return pl.pallas_call(
        functools.partial(_attn_kernel, tk=tk),
        grid=(b, h, s // tq),
        in_specs=[
            pl.BlockSpec((1, 1, tq, QK_PAD), lambda bi, hi, i: (bi, hi, i, 0)),
            pl.BlockSpec((1, 1, s, QK_PAD), lambda bi, hi, i: (bi, hi, 0, 0)),
            pl.BlockSpec((1, 1, s, VDIM), lambda bi, hi, i: (bi, hi, 0, 0)),
        ],
        out_specs=pl.BlockSpec((1, tq, VDIM), lambda bi, hi, i: (bi, i, hi)),
        out_shape=jax.ShapeDtypeStruct((b, s, h * VDIM), _BF),
        compiler_params=_cparams(("arbitrary", "arbitrary", "arbitrary")),
        name="attention",
    )(q, k, v)


def _post_kernel(o_ref, ysg_ref, x_ref, mod_ref, gao_ref, wo_ref, g2_ref, wq_ref, keys_ref,
                 x1_ref, h2_ref, sc_ref):
    gate1 = mod_ref[0, 2:3, :]
    shift2 = mod_ref[0, 3:4, :]
    scale2 = mod_ref[0, 4:5, :]
    width = o_ref.shape[1]
    yn = _rms(o_ref[...].astype(_F32), gao_ref[...]).astype(_BF)
    y = _dot(yn, wo_ref[0:width, :]) + _dot(ysg_ref[...], wo_ref[width:, :])
    x1 = x_ref[...] + gate1 * y
    x1_ref[...] = x1
    h2 = (_rms(x1, g2_ref[...]) * (1.0 + scale2) + shift2).astype(_BF)
    h2_ref[...] = h2
    qp = _dot(h2, wq_ref[...]).astype(_BF)
    for hh in range(2 * PEER_HEADS):
        sc_ref[hh] = _dot_nt(keys_ref[hh], qp[:, hh * HALF_DIM:(hh + 1) * HALF_DIM])


def _post(o, ysg, x2d, mod3, gao, wo, g2, wq, keys, tt, s):
    t, d = x2d.shape
    width = o.shape[1]
    per_b = s // tt
    full = lambda *shape: pl.BlockSpec(shape, lambda i: (0,) * len(shape))
    return pl.pallas_call(
        _post_kernel,
        grid=(t // tt,),
        in_specs=[
            pl.BlockSpec((tt, width), lambda i: (i, 0)),
            pl.BlockSpec((tt, width), lambda i: (i, 0)),
            pl.BlockSpec((tt, d), lambda i: (i, 0)),
            pl.BlockSpec((1, 6, d), lambda i: (i // per_b, 0, 0)),
            full(1, width),
            full(2 * width, d),
            full(1, d),
            full(d, wq.shape[1]),
            full(2 * PEER_HEADS, N_KEYS, HALF_DIM),
        ],
        out_specs=[
            pl.BlockSpec((tt, d), lambda i: (i, 0)),
            pl.BlockSpec((tt, d), lambda i: (i, 0)),
            pl.BlockSpec((2 * PEER_HEADS, N_KEYS, tt), lambda i: (0, 0, i)),
        ],
        out_shape=[
            jax.ShapeDtypeStruct((t, d), _F32),
            jax.ShapeDtypeStruct((t, d), _BF),
            jax.ShapeDtypeStruct((2 * PEER_HEADS, N_KEYS, t), _F32),
        ],
        compiler_params=_cparams(("arbitrary",)),
        name="post",
    )(o, ysg, x2d, mod3, gao, wo, g2, wq, keys)


_NB = [TOPK // (a + 1) for a in range(TOPK)]


def _top16(sv):
    n = sv.shape[1]
    key_iota = lax.broadcasted_iota(jnp.int32, sv.shape, 0)
    row_iota = lax.broadcasted_iota(jnp.int32, (TOPK, n), 0)
    vals = jnp.zeros((TOPK, n), _F32)
    idxs = jnp.zeros((TOPK, n), jnp.int32)
    for r in range(TOPK):
        m = jnp.max(sv, axis=0, keepdims=True)
        ix = jnp.min(jnp.where(sv == m, key_iota, N_KEYS), axis=0, keepdims=True)
        vals = jnp.where(row_iota == r, m, vals)
        idxs = jnp.where(row_iota == r, ix, idxs)
        sv = jnp.where(key_iota == ix, -jnp.inf, sv)
    return vals, idxs


def _pair_top16(v1, i1, v2, i2):
    n = v1.shape[1]
    neg = -jnp.inf
    sub_iota = lax.broadcasted_iota(jnp.int32, (SUBLANES, n), 0)
    row_iota = lax.broadcasted_iota(jnp.int32, (TOPK, n), 0)
    blocks = []
    for t in range(2):
        blocks.append((v1[0:1] + v2[t * 8:(t + 1) * 8], sub_iota + t * 8))
    for a in range(1, 8):
        val = jnp.where(sub_iota < _NB[a], v1[a:a + 1] + v2[0:8], neg)
        blocks.append((val, sub_iota + a * TOPK))
    blocks.append((v1[8:16] + v2[0:1], (sub_iota + 8) * TOPK))

    sc = jnp.zeros((TOPK, n), _F32)
    rows = jnp.zeros((TOPK, n), jnp.int32)
    cols = jnp.zeros((TOPK, n), jnp.int32)
    big = TOPK * TOPK
    for r in range(TOPK):
        m = blocks[0][0]
        for val, _ in blocks[1:]:
            m = jnp.maximum(m, val)
        m = jnp.max(m, axis=0, keepdims=True)
        fi = jnp.where(blocks[0][0] == m, blocks[0][1], big)
        for val, flat in blocks[1:]:
            fi = jnp.minimum(fi, jnp.where(val == m, flat, big))
        fi = jnp.min(fi, axis=0, keepdims=True)
        blocks = [(jnp.where(flat == fi, neg, val), flat) for val, flat in blocks]
        a_sel = jnp.right_shift(fi, 4)
        b_sel = jnp.bitwise_and(fi, TOPK - 1)
        ri = jnp.sum(jnp.where(row_iota == a_sel, i1, 0), axis=0, keepdims=True)
        ci = jnp.sum(jnp.where(row_iota == b_sel, i2, 0), axis=0, keepdims=True)
        sc = jnp.where(row_iota == r, m, sc)
        rows = jnp.where(row_iota == r, ri, rows)
        cols = jnp.where(row_iota == r, ci, cols)
    return sc, rows, cols


def _route_kernel(sc_ref, rows_ref, cols_ref, gate_ref):
    rows_all, cols_all, gate_all = [], [], []
    for hd in range(PEER_HEADS):
        v1, i1 = _top16(sc_ref[2 * hd])
        v2, i2 = _top16(sc_ref[2 * hd + 1])
        sc, rows, cols = _pair_top16(v1, i1, v2, i2)
        e = jnp.exp(sc - sc[0:1])
        gate_all.append(e / jnp.sum(e, axis=0, keepdims=True))
        rows_all.append(rows)
        cols_all.append(cols)
    rows_ref[...] = jnp.concatenate(rows_all, axis=0).T
    cols_ref[...] = jnp.concatenate(cols_all, axis=0).T
    gate_ref[...] = jnp.concatenate(gate_all, axis=0).T


def _route(scores, tn):
    hh, nk, t = scores.shape
    return pl.pallas_call(
        _route_kernel,
        grid=(t // tn,),
        in_specs=[pl.BlockSpec((hh, nk, tn), lambda i: (0, 0, i))],
        out_specs=[pl.BlockSpec((tn, N_SEL), lambda i: (i, 0))] * 3,
        out_shape=[
            jax.ShapeDtypeStruct((t, N_SEL), jnp.int32),
            jax.ShapeDtypeStruct((t, N_SEL), jnp.int32),
            jax.ShapeDtypeStruct((t, N_SEL), _F32),
        ],
        compiler_params=_cparams(("arbitrary",)),
        name="route",
    )(scores)


def _peer_kernel(h2_ref, rows_ref, cols_ref, gate_ref, u_ref, v_ref, x1_ref, mod_ref, gf_ref,
                 o_ref, g_scr, acc_scr, *, sub):
    tb = h2_ref.shape[0]
    eb = u_ref.shape[0]
    j = pl.program_id(1)

    @pl.when(j == 0)
    def _build_gates():
        acc_scr[...] = jnp.zeros_like(acc_scr)
        key_iota = lax.broadcasted_iota(jnp.int32, (N_KEYS, N_SEL), 0)

        def one(t, c):
            r = rows_ref[pl.ds(t, 1), :]
            cidx = cols_ref[pl.ds(t, 1), :]
            g = gate_ref[pl.ds(t, 1), :]
            a = jnp.where(key_iota == r, g, 0.0).astype(_BF)
            bmat = jnp.where(key_iota == cidx, 1.0, 0.0).astype(_BF)
            base = pl.multiple_of(t * G_PITCH, SUBLANES)
            g_scr[pl.ds(base, N_KEYS), :] = _dot_nt(a, bmat)
            return c

        lax.fori_loop(0, tb, one, 0, unroll=2)

    h2 = h2_ref[...]
    rows_per = sub // N_KEYS
    for sb in range(eb // sub):
        a = _dot_nt(h2, u_ref[sb * sub:(sb + 1) * sub, :])
        gparts = []
        for ri in range(rows_per):
            row = j * (eb // N_KEYS) + sb * rows_per + ri
            gparts.append(g_scr[pl.ds(row, tb, stride=G_PITCH), :])
        gt = gparts[0] if rows_per == 1 else jnp.concatenate(gparts, axis=-1)
        w = (_gelu(a) * gt).astype(_BF)
        acc_scr[...] += _dot(w, v_ref[sb * sub:(sb + 1) * sub, :])

    @pl.when(j == pl.num_programs(1) - 1)
    def _finish():
        gate2 = mod_ref[0, 5:6, :]
        x2 = x1_ref[...] + gate2 * acc_scr[...]
        o_ref[...] = _rms(x2, gf_ref[...])


def _peer(h2, rows, cols, gates, u_bf, v_bf, x1, mod3, gf, tb, eb, sub, s):
    t, d = h2.shape
    e = u_bf.shape[0]
    per_b = s // tb
    return pl.pallas_call(
        functools.partial(_peer_kernel, sub=sub),
        grid=(t // tb, e // eb),
        in_specs=[
            pl.BlockSpec((tb, d), lambda i, j: (i, 0)),
            pl.BlockSpec((tb, N_SEL), lambda i, j: (i, 0)),
            pl.BlockSpec((tb, N_SEL), lambda i, j: (i, 0)),
            pl.BlockSpec((tb, N_SEL), lambda i, j: (i, 0)),
            pl.BlockSpec((eb, d), lambda i, j: (j, 0)),
            pl.BlockSpec((eb, d), lambda i, j: (j, 0)),
            pl.BlockSpec((tb, d), lambda i, j: (i, 0)),
            pl.BlockSpec((1, 6, d), lambda i, j: (i // per_b, 0, 0)),
            pl.BlockSpec((1, d), lambda i, j: (0, 0)),
        ],
        out_specs=pl.BlockSpec((tb, d), lambda i, j: (i, 0)),
        out_shape=jax.ShapeDtypeStruct((t, d), _F32),
        scratch_shapes=[
            pltpu.VMEM((tb * G_PITCH, N_KEYS), _F32),
            pltpu.VMEM((tb, d), _F32),
        ],
        compiler_params=_cparams(("arbitrary", "arbitrary")),
        name="peer",
    )(h2, rows, cols, gates, u_bf, v_bf, x1, mod3, gf)


def _pad_cols(w, width):
    return jnp.pad(w, ((0, 0), (0, width - w.shape[1])))


def _rot_cols(w):
    half = w.shape[1] // 2
    return jnp.concatenate([-w[:, half:], w[:, :half]], axis=1)


def kernel(x, c, positions, w_ada, b_ada, g_norm1, w_in, g_q_a, w_uq, g_kv_a, w_ukv, g_sg, w_sg, b_sg,
           g_attn_out, g_sg_out, w_o, g_norm2, w_peer_q, peer_keys, peer_u, peer_v, g_final):
    b, s, d = x.shape
    depth = w_ada.shape[0]
    assert depth == 1, "the final norm is fused into the last layer's expert kernel"
    n_mod = w_ada.shape[2] // d
    t = b * s

    inv_freq = 1.0 / (ROPE_THETA ** (jnp.arange(0, ROPE, 2, dtype=_F32) / ROPE))
    invf = _pad_cols(jnp.concatenate([inv_freq, inv_freq])[None, :], LANES)
    pos3 = positions[:, :, None]
    c_pad = jnp.pad(c, ((0, SUBLANES - b), (0, 0)))

    o1 = Q_RANK
    o2 = o1 + KV_RANK
    o3 = o2 + ROPE
    xs = x
    for l in range(depth):
        mod = _adaln(c_pad, w_ada[l], b_ada[l][None, :])[:b]
        mod3 = mod.reshape(b, n_mod, d)

        w_kr = w_in[l][:, o2:o3]
        win_ext = jnp.concatenate(
            [w_in[l][:, :o2], _pad_cols(w_kr, LANES), _pad_cols(_rot_cols(w_kr), LANES), w_in[l][:, o3:]],
            axis=1).astype(_BF)
        wq3 = w_uq[l].reshape(Q_RANK, HEADS, QK_DIM)
        wq_nope = wq3[:, :, :NOPE].reshape(Q_RANK, HEADS * NOPE)
        wq_rope = [wq3[:, hd, NOPE:] for hd in range(HEADS)]
        wuq_ext = jnp.concatenate(
            [wq_nope] + [_pad_cols(w, LANES) for w in wq_rope]
            + [_pad_cols(_rot_cols(w), LANES) for w in wq_rope], axis=1).astype(_BF)

        q, k, v, ysg = _inproj(
            xs, pos3, mod3, g_norm1[l][None, :], win_ext, g_q_a[l][None, :], wuq_ext,
            g_kv_a[l][None, :], w_ukv[l].astype(_BF), invf, g_sg[l][None, :], w_sg[l].astype(_BF),
            b_sg[l][:, :, None], g_sg_out[l][None, :], ts=512)
        o = _attention(q, k, v, tq=256, tk=512)

        keys = peer_keys[l].reshape(2 * PEER_HEADS, N_KEYS, HALF_DIM).astype(_BF)
        x1, h2, scores = _post(
            o.reshape(t, HEADS * VDIM), ysg.reshape(t, SG_WIDTH), xs.reshape(t, d), mod3,
            g_attn_out[l][None, :], w_o[l].astype(_BF), g_norm2[l][None, :], w_peer_q[l].astype(_BF),
            keys, tt=256, s=s)
        rows, cols, gates = _route(scores, tn=128)
        out = _peer(h2, rows, cols, gates, peer_u[l].astype(_BF), peer_v[l].astype(_BF), x1, mod3,
                    g_final[None, :], tb=256, eb=2048, sub=512, s=s)
        xs = out.reshape(b, s, d)
    return xs
```

```python
import functools
import math

import jax
import jax.numpy as jnp
from jax import lax
from jax.experimental import pallas as pl
from jax.experimental.pallas import tpu as pltpu

LANES = 128
SUBLANES = 8
VMEM_LIMIT_BYTES = 56 * 1024 * 1024

EPS = 1e-6
ROPE_THETA = 10000.0
HEADS = 4
NOPE = 128
ROPE = 64
VDIM = 128
QK_DIM = NOPE + ROPE
QK_PAD = 2 * LANES
Q_RANK = 256
KV_RANK = 128
SG_HEADS = 4
SG_DIM = 128
SG_CHUNK = 128
SG_WIDTH = SG_HEADS * SG_DIM
PEER_HEADS = 8
N_KEYS = 128
TOPK = 16
HALF_DIM = 128
N_SEL = PEER_HEADS * TOPK

G_PITCH = N_KEYS + SUBLANES

_BF = jnp.bfloat16
_F32 = jnp.float32


def _cparams(sem):
    return pltpu.CompilerParams(dimension_semantics=sem, vmem_limit_bytes=VMEM_LIMIT_BYTES)


def _rms(x, g):
    return x * lax.rsqrt(jnp.mean(x * x, axis=-1, keepdims=True) + EPS) * g


def _gelu(x):
    return 0.5 * x * (1.0 + lax.erf(x * (1.0 / math.sqrt(2.0))))


def _dot(a, b):
    return jnp.dot(a, b, preferred_element_type=_F32)


def _dot_nt(a, b):
    return lax.dot_general(a, b, (((1,), (1,)), ((), ())), preferred_element_type=_F32)


def _adaln_kernel(c_ref, w_ref, b_ref, o_ref):
    c = c_ref[...]
    c_act = (c * jax.nn.sigmoid(c)).astype(_BF)
    o_ref[...] = _dot(c_act, w_ref[...].astype(_BF)) + b_ref[...]


def _adaln(c_pad, w_ada, b_ada):
    rows, d = c_pad.shape
    n = w_ada.shape[1]
    tn = 1024
    return pl.pallas_call(
        _adaln_kernel,
        grid=(n // tn,),
        in_specs=[
            pl.BlockSpec((rows, d), lambda j: (0, 0)),
            pl.BlockSpec((d, tn), lambda j: (0, j)),
            pl.BlockSpec((1, tn), lambda j: (0, j)),
        ],
        out_specs=pl.BlockSpec((rows, tn), lambda j: (0, j)),
        out_shape=jax.ShapeDtypeStruct((rows, n), _F32),
        compiler_params=_cparams(("arbitrary",)),
        name="adaln",
    )(c_pad, w_ada, b_ada)


def _inproj_kernel(x_ref, pos_ref, mod_ref, g1_ref, win_ref, gq_ref, wuq_ref, gkv_ref, wukv_ref,
                   invf_ref, gsg_ref, wsg_ref, bsg_ref, gsgo_ref,
                   q_ref, k_ref, v_ref, ysg_ref, *, scale):
    ts = x_ref.shape[1]
    x = x_ref[0]
    shift1 = mod_ref[0, 0:1, :]
    scale1 = mod_ref[0, 1:2, :]
    h = _rms(x, g1_ref[...]) * (1.0 + scale1) + shift1
    z = _dot(h.astype(_BF), win_ref[...])

    o_kv = Q_RANK
    o_kr = o_kv + KV_RANK
    o_krot = o_kr + LANES
    o_u = o_krot + LANES
    o_v = o_u + SG_WIDTH

    ang = pos_ref[0].astype(_F32) * invf_ref[...]
    cos2 = jnp.cos(ang)
    sin2 = jnp.sin(ang)

    cq = _rms(z[:, :Q_RANK], gq_ref[...]).astype(_BF)
    qall = _dot(cq, wuq_ref[...])
    for hd in range(HEADS):
        q_nope = qall[:, hd * NOPE:(hd + 1) * NOPE]
        q_r = qall[:, HEADS * NOPE + hd * LANES: HEADS * NOPE + (hd + 1) * LANES]
        q_rot = qall[:, HEADS * (NOPE + LANES) + hd * LANES: HEADS * (NOPE + LANES) + (hd + 1) * LANES]
        q_ref[0, hd, :, 0:LANES] = (q_nope * scale).astype(_BF)
        q_ref[0, hd, :, LANES:QK_PAD] = ((q_r * cos2 + q_rot * sin2) * scale).astype(_BF)

    ckv = _rms(z[:, o_kv:o_kr], gkv_ref[...]).astype(_BF)
    kv = _dot(ckv, wukv_ref[...])
    k_rope = (z[:, o_kr:o_krot] * cos2 + z[:, o_krot:o_u] * sin2).astype(_BF)
    for hd in range(HEADS):
        base = hd * (NOPE + VDIM)
        k_ref[0, hd, :, 0:LANES] = kv[:, base:base + NOPE].astype(_BF)
        k_ref[0, hd, :, LANES:QK_PAD] = k_rope
        v_ref[0, hd] = kv[:, base + NOPE:base + NOPE + VDIM].astype(_BF)

    u = _gelu(z[:, o_u:o_v])
    vg = _rms(_gelu(z[:, o_v:o_v + SG_WIDTH]), gsg_ref[...]).astype(_BF)
    for ch in range(ts // SG_CHUNK):
        r0 = ch * SG_CHUNK
        parts = []
        for hd in range(SG_HEADS):
            c0 = hd * SG_DIM
            mix = _dot(wsg_ref[hd], vg[r0:r0 + SG_CHUNK, c0:c0 + SG_DIM]) + bsg_ref[hd]
            parts.append(u[r0:r0 + SG_CHUNK, c0:c0 + SG_DIM] * mix)
        y = jnp.concatenate(parts, axis=-1)
        ysg_ref[0, r0:r0 + SG_CHUNK, :] = _rms(y, gsgo_ref[...]).astype(_BF)


def _inproj(x, pos3, mod3, g1, win_ext, gq, wuq_ext, gkv, wukv, invf, gsg, wsg, bsg3, gsgo, ts):
    b, s, d = x.shape
    nz = win_ext.shape[1]
    full = lambda *shape: pl.BlockSpec(shape, lambda bi, i: (0,) * len(shape))
    return pl.pallas_call(
        functools.partial(_inproj_kernel, scale=QK_DIM ** -0.5 * math.log2(math.e)),
        grid=(b, s // ts),
        in_specs=[
            pl.BlockSpec((1, ts, d), lambda bi, i: (bi, i, 0)),
            pl.BlockSpec((1, ts, 1), lambda bi, i: (bi, i, 0)),
            pl.BlockSpec((1, 6, d), lambda bi, i: (bi, 0, 0)),
            full(1, d),
            full(d, nz),
            full(1, Q_RANK),
            full(Q_RANK, wuq_ext.shape[1]),
            full(1, KV_RANK),
            full(KV_RANK, wukv.shape[1]),
            full(1, LANES),
            full(1, SG_WIDTH),
            full(SG_HEADS, SG_CHUNK, SG_CHUNK),
            full(SG_HEADS, SG_CHUNK, 1),
            full(1, SG_WIDTH),
        ],
        out_specs=[
            pl.BlockSpec((1, HEADS, ts, QK_PAD), lambda bi, i: (bi, 0, i, 0)),
            pl.BlockSpec((1, HEADS, ts, QK_PAD), lambda bi, i: (bi, 0, i, 0)),
            pl.BlockSpec((1, HEADS, ts, VDIM), lambda bi, i: (bi, 0, i, 0)),
            pl.BlockSpec((1, ts, SG_WIDTH), lambda bi, i: (bi, i, 0)),
        ],
        out_shape=[
            jax.ShapeDtypeStruct((b, HEADS, s, QK_PAD), _BF),
            jax.ShapeDtypeStruct((b, HEADS, s, QK_PAD), _BF),
            jax.ShapeDtypeStruct((b, HEADS, s, VDIM), _BF),
            jax.ShapeDtypeStruct((b, s, SG_WIDTH), _BF),
        ],
        compiler_params=_cparams(("arbitrary", "arbitrary")),
        name="inproj",
    )(x, pos3, mod3, g1, win_ext, gq, wuq_ext, gkv, wukv, invf, gsg, wsg, bsg3, gsgo)


def _attn_kernel(q_ref, k_ref, v_ref, o_ref, *, tk):
    nk = k_ref.shape[2] // tk
    tq = q_ref.shape[2]
    q = q_ref[0, 0]
    ones = jnp.ones((tk, LANES), _BF)

    def scores(j):
        return _dot_nt(q, k_ref[0, 0, pl.ds(pl.multiple_of(j * tk, tk), tk), :])

    def step(j, m, acc, sc):
        m_new = jnp.maximum(m, jnp.max(sc, axis=-1, keepdims=True))
        alpha = jnp.exp2(m - m_new)
        p = jnp.exp2(sc - m_new).astype(_BF)
        vc = v_ref[0, 0, pl.ds(pl.multiple_of(j * tk, tk), tk), :]
        pv = _dot(p, jnp.concatenate([vc, ones], axis=-1))
        return m_new, alpha * acc + pv

    def body(j, carry):
        m, acc, sc = carry
        sc_next = scores(j + 1)
        m, acc = step(j, m, acc, sc)
        return m, acc, sc_next

    m0 = jnp.full((tq, 1), -jnp.inf, _F32)
    a0 = jnp.zeros((tq, VDIM + LANES), _F32)
    m, acc, sc = lax.fori_loop(0, nk - 1, body, (m0, a0, scores(0)))
    _, acc = step(nk - 1, m, acc, sc)
    o_ref[0] = (acc[:, :VDIM] / acc[:, VDIM:]).astype(_BF)


def _attention(q, k, v, tq, tk):
    b, h, s, _ = q.shape
    return pl.pallas_call(
        functools.partial(_attn_kernel, tk=tk),
        grid=(b, h, s // tq),
        in_specs=[
            pl.BlockSpec((1, 1, tq, QK_PAD), lambda bi, hi, i: (bi, hi, i, 0)),
            pl.BlockSpec((1, 1, s, QK_PAD), lambda bi, hi, i: (bi, hi, 0, 0)),
            pl.BlockSpec((1, 1, s, VDIM), lambda bi, hi, i: (bi, hi, 0, 0)),
        ],
        out_specs=pl.BlockSpec((1, tq, VDIM), lambda bi, hi, i: (bi, i, hi)),
        out_shape=jax.ShapeDtypeStruct((b, s, h * VDIM), _BF),
        compiler_params=_cparams(("arbitrary", "arbitrary", "arbitrary")),
        name="attention",
    )(q, k, v)


def _post_kernel(o_ref, ysg_ref, x_ref, mod_ref, gao_ref, wo_ref, g2_ref, wq_ref, keys_ref,
                 x1_ref, h2_ref, sc_ref):
    gate1 = mod_ref[0, 2:3, :]
    shift2 = mod_ref[0, 3:4, :]
    scale2 = mod_ref[0, 4:5, :]
    width = o_ref.shape[1]
    yn = _rms(o_ref[...].astype(_F32), gao_ref[...]).astype(_BF)
    y = _dot(yn, wo_ref[0:width, :]) + _dot(ysg_ref[...], wo_ref[width:, :])
    x1 = x_ref[...] + gate1 * y
    x1_ref[...] = x1
    h2 = (_rms(x1, g2_ref[...]) * (1.0 + scale2) + shift2).astype(_BF)
    h2_ref[...] = h2
    qp = _dot(h2, wq_ref[...]).astype(_BF)
    for hh in range(2 * PEER_HEADS):
        sc_ref[hh] = _dot_nt(keys_ref[hh], qp[:, hh * HALF_DIM:(hh + 1) * HALF_DIM])


def _post(o, ysg, x2d, mod3, gao, wo, g2, wq, keys, tt, s):
    t, d = x2d.shape
    width = o.shape[1]
    per_b = s // tt
    full = lambda *shape: pl.BlockSpec(shape, lambda i: (0,) * len(shape))
    return pl.pallas_call(
        _post_kernel,
        grid=(t // tt,),
        in_specs=[
            pl.BlockSpec((tt, width), lambda i: (i, 0)),
            pl.BlockSpec((tt, width), lambda i: (i, 0)),
            pl.BlockSpec((tt, d), lambda i: (i, 0)),
            pl.BlockSpec((1, 6, d), lambda i: (i // per_b, 0, 0)),
            full(1, width),
            full(2 * width, d),
            full(1, d),
            full(d, wq.shape[1]),
            full(2 * PEER_HEADS, N_KEYS, HALF_DIM),
        ],
        out_specs=[
            pl.BlockSpec((tt, d), lambda i: (i, 0)),
            pl.BlockSpec((tt, d), lambda i: (i, 0)),
            pl.BlockSpec((2 * PEER_HEADS, N_KEYS, tt), lambda i: (0, 0, i)),
        ],
        out_shape=[
            jax.ShapeDtypeStruct((t, d), _F32),
            jax.ShapeDtypeStruct((t, d), _BF),
            jax.ShapeDtypeStruct((2 * PEER_HEADS, N_KEYS, t), _F32),
        ],
        compiler_params=_cparams(("arbitrary",)),
        name="post",
    )(o, ysg, x2d, mod3, gao, wo, g2, wq, keys)


_NB = [TOPK // (a + 1) for a in range(TOPK)]


def _top16(sv):
    n = sv.shape[1]
    key_iota = lax.broadcasted_iota(jnp.int32, sv.shape, 0)
    row_iota = lax.broadcasted_iota(jnp.int32, (TOPK, n), 0)
    vals = jnp.zeros((TOPK, n), _F32)
    idxs = jnp.zeros((TOPK, n), jnp.int32)
    for r in range(TOPK):
        m = jnp.max(sv, axis=0, keepdims=True)
        ix = jnp.min(jnp.where(sv == m, key_iota, N_KEYS), axis=0, keepdims=True)
        vals = jnp.where(row_iota == r, m, vals)
        idxs = jnp.where(row_iota == r, ix, idxs)
        sv = jnp.where(key_iota == ix, -jnp.inf, sv)
    return vals, idxs


def _pair_top16(v1, i1, v2, i2):
    n = v1.shape[1]
    neg = -jnp.inf
    sub_iota = lax.broadcasted_iota(jnp.int32, (SUBLANES, n), 0)
    row_iota = lax.broadcasted_iota(jnp.int32, (TOPK, n), 0)
    blocks = []
    for t in range(2):
        blocks.append((v1[0:1] + v2[t * 8:(t + 1) * 8], sub_iota + t * 8))
    for a in range(1, 8):
        val = jnp.where(sub_iota < _NB[a], v1[a:a + 1] + v2[0:8], neg)
        blocks.append((val, sub_iota + a * TOPK))
    blocks.append((v1[8:16] + v2[0:1], (sub_iota + 8) * TOPK))

    sc = jnp.zeros((TOPK, n), _F32)
    rows = jnp.zeros((TOPK, n), jnp.int32)
    cols = jnp.zeros((TOPK, n), jnp.int32)
    big = TOPK * TOPK
    for r in range(TOPK):
        m = blocks[0][0]
        for val, _ in blocks[1:]:
            m = jnp.maximum(m, val)
        m = jnp.max(m, axis=0, keepdims=True)
        fi = jnp.where(blocks[0][0] == m, blocks[0][1], big)
        for val, flat in blocks[1:]:
            fi = jnp.minimum(fi, jnp.where(val == m, flat, big))
        fi = jnp.min(fi, axis=0, keepdims=True)
        blocks = [(jnp.where(flat == fi, neg, val), flat) for val, flat in blocks]
        a_sel = jnp.right_shift(fi, 4)
        b_sel = jnp.bitwise_and(fi, TOPK - 1)
        ri = jnp.sum(jnp.where(row_iota == a_sel, i1, 0), axis=0, keepdims=True)
        ci = jnp.sum(jnp.where(row_iota == b_sel, i2, 0), axis=0, keepdims=True)
        sc = jnp.where(row_iota == r, m, sc)
        rows = jnp.where(row_iota == r, ri, rows)
        cols = jnp.where(row_iota == r, ci, cols)
    return sc, rows, cols


def _route_kernel(sc_ref, rows_ref, cols_ref, gate_ref):
    rows_all, cols_all, gate_all = [], [], []
    for hd in range(PEER_HEADS):
        v1, i1 = _top16(sc_ref[2 * hd])
        v2, i2 = _top16(sc_ref[2 * hd + 1])
        sc, rows, cols = _pair_top16(v1, i1, v2, i2)
        e = jnp.exp(sc - sc[0:1])
        gate_all.append(e / jnp.sum(e, axis=0, keepdims=True))
        rows_all.append(rows)
        cols_all.append(cols)
    rows_ref[...] = jnp.concatenate(rows_all, axis=0).T
    cols_ref[...] = jnp.concatenate(cols_all, axis=0).T
    gate_ref[...] = jnp.concatenate(gate_all, axis=0).T


def _route(scores, tn):
    hh, nk, t = scores.shape
    return pl.pallas_call(
        _route_kernel,
        grid=(t // tn,),
        in_specs=[pl.BlockSpec((hh, nk, tn), lambda i: (0, 0, i))],
        out_specs=[pl.BlockSpec((tn, N_SEL), lambda i: (i, 0))] * 3,
        out_shape=[
            jax.ShapeDtypeStruct((t, N_SEL), jnp.int32),
            jax.ShapeDtypeStruct((t, N_SEL), jnp.int32),
            jax.ShapeDtypeStruct((t, N_SEL), _F32),
        ],
        compiler_params=_cparams(("arbitrary",)),
        name="route",
    )(scores)


def _peer_kernel(h2_ref, rows_ref, cols_ref, gate_ref, u_ref, v_ref, x1_ref, mod_ref, gf_ref,
                 o_ref, g_scr, acc_scr, *, sub):
    tb = h2_ref.shape[0]
    eb = u_ref.shape[0]
    j = pl.program_id(1)

    @pl.when(j == 0)
    def _build_gates():
        acc_scr[...] = jnp.zeros_like(acc_scr)
        key_iota = lax.broadcasted_iota(jnp.int32, (N_KEYS, N_SEL), 0)

        def one(t, c):
            r = rows_ref[pl.ds(t, 1), :]
            cidx = cols_ref[pl.ds(t, 1), :]
            g = gate_ref[pl.ds(t, 1), :]
            a = jnp.where(key_iota == r, g, 0.0).astype(_BF)
            bmat = jnp.where(key_iota == cidx, 1.0, 0.0).astype(_BF)
            base = pl.multiple_of(t * G_PITCH, SUBLANES)
            g_scr[pl.ds(base, N_KEYS), :] = _dot_nt(a, bmat)
            return c

        lax.fori_loop(0, tb, one, 0, unroll=16)

    h2 = h2_ref[...]
    rows_per = sub // N_KEYS
    for sb in range(eb // sub):
        a = _dot_nt(h2, u_ref[sb * sub:(sb + 1) * sub, :])
        gparts = []
        for ri in range(rows_per):
            row = j * (eb // N_KEYS) + sb * rows_per + ri
            gparts.append(g_scr[pl.ds(row, tb, stride=G_PITCH), :])
        gt = gparts[0] if rows_per == 1 else jnp.concatenate(gparts, axis=-1)
        w = (_gelu(a) * gt).astype(_BF)
        acc_scr[...] += _dot(w, v_ref[sb * sub:(sb + 1) * sub, :])

    @pl.when(j == pl.num_programs(1) - 1)
    def _finish():
        gate2 = mod_ref[0, 5:6, :]
        x2 = x1_ref[...] + gate2 * acc_scr[...]
        o_ref[...] = _rms(x2, gf_ref[...])


def _peer(h2, rows, cols, gates, u_bf, v_bf, x1, mod3, gf, tb, eb, sub, s):
    t, d = h2.shape
    e = u_bf.shape[0]
    per_b = s // tb
    return pl.pallas_call(
        functools.partial(_peer_kernel, sub=sub),
        grid=(t // tb, e // eb),
        in_specs=[
            pl.BlockSpec((tb, d), lambda i, j: (i, 0)),
            pl.BlockSpec((tb, N_SEL), lambda i, j: (i, 0)),
            pl.BlockSpec((tb, N_SEL), lambda i, j: (i, 0)),
            pl.BlockSpec((tb, N_SEL), lambda i, j: (i, 0)),
            pl.BlockSpec((eb, d), lambda i, j: (j, 0)),
            pl.BlockSpec((eb, d), lambda i, j: (j, 0)),
            pl.BlockSpec((tb, d), lambda i, j: (i, 0)),
            pl.BlockSpec((1, 6, d), lambda i, j: (i // per_b, 0, 0)),
            pl.BlockSpec((1, d), lambda i, j: (0, 0)),
        ],
        out_specs=pl.BlockSpec((tb, d), lambda i, j: (i, 0)),
        out_shape=jax.ShapeDtypeStruct((t, d), _F32),
        scratch_shapes=[
            pltpu.VMEM((tb * G_PITCH, N_KEYS), _F32),
            pltpu.VMEM((tb, d), _F32),
        ],
        compiler_params=_cparams(("arbitrary", "arbitrary")),
        name="peer",
    )(h2, rows, cols, gates, u_bf, v_bf, x1, mod3, gf)


def _pad_cols(w, width):
    return jnp.pad(w, ((0, 0), (0, width - w.shape[1])))


def _rot_cols(w):
    half = w.shape[1] // 2
    return jnp.concatenate([-w[:, half:], w[:, :half]], axis=1)


def kernel(x, c, positions, w_ada, b_ada, g_norm1, w_in, g_q_a, w_uq, g_kv_a, w_ukv, g_sg, w_sg, b_sg,
           g_attn_out, g_sg_out, w_o, g_norm2, w_peer_q, peer_keys, peer_u, peer_v, g_final):
    b, s, d = x.shape
    depth = w_ada.shape[0]
    assert depth == 1, "the final norm is fused into the last layer's expert kernel"
    n_mod = w_ada.shape[2] // d
    t = b * s

    inv_freq = 1.0 / (ROPE_THETA ** (jnp.arange(0, ROPE, 2, dtype=_F32) / ROPE))
    invf = _pad_cols(jnp.concatenate([inv_freq, inv_freq])[None, :], LANES)
    pos3 = positions[:, :, None]
    c_pad = jnp.pad(c, ((0, SUBLANES - b), (0, 0)))

    o1 = Q_RANK
    o2 = o1 + KV_RANK
    o3 = o2 + ROPE
    xs = x
    for l in range(depth):
        mod = _adaln(c_pad, w_ada[l], b_ada[l][None, :])[:b]
        mod3 = mod.reshape(b, n_mod, d)

        w_kr = w_in[l][:, o2:o3]
        win_ext = jnp.concatenate(
            [w_in[l][:, :o2], _pad_cols(w_kr, LANES), _pad_cols(_rot_cols(w_kr), LANES), w_in[l][:, o3:]],
            axis=1).astype(_BF)
        wq3 = w_uq[l].reshape(Q_RANK, HEADS, QK_DIM)
        wq_nope = wq3[:, :, :NOPE].reshape(Q_RANK, HEADS * NOPE)
        wq_rope = [wq3[:, hd, NOPE:] for hd in range(HEADS)]
        wuq_ext = jnp.concatenate(
            [wq_nope] + [_pad_cols(w, LANES) for w in wq_rope]
            + [_pad_cols(_rot_cols(w), LANES) for w in wq_rope], axis=1).astype(_BF)

        q, k, v, ysg = _inproj(
            xs, pos3, mod3, g_norm1[l][None, :], win_ext, g_q_a[l][None, :], wuq_ext,
            g_kv_a[l][None, :], w_ukv[l].astype(_BF), invf, g_sg[l][None, :], w_sg[l].astype(_BF),
            b_sg[l][:, :, None], g_sg_out[l][None, :], ts=512)
        o = _attention(q, k, v, tq=512, tk=512)

        keys = peer_keys[l].reshape(2 * PEER_HEADS, N_KEYS, HALF_DIM).astype(_BF)
        x1, h2, scores = _post(
            o.reshape(t, HEADS * VDIM), ysg.reshape(t, SG_WIDTH), xs.reshape(t, d), mod3,
            g_attn_out[l][None, :], w_o[l].astype(_BF), g_norm2[l][None, :], w_peer_q[l].astype(_BF),
            keys, tt=256, s=s)
        rows, cols, gates = _route(scores, tn=128)
        out = _peer(h2, rows, cols, gates, peer_u[l].astype(_BF), peer_v[l].astype(_BF), x1, mod3,
                    g_final[None, :], tb=256, eb=2048, sub=512, s=s)
        xs = out.reshape(b, s, d)
    return xs
```

```python
import functools
import math

import jax
import jax.numpy as jnp
from jax import lax
from jax.experimental import pallas as pl
from jax.experimental.pallas import tpu as pltpu

LANES = 128
SUBLANES = 8
VMEM_LIMIT_BYTES = 56 * 1024 * 1024

EPS = 1e-6
ROPE_THETA = 10000.0
HEADS = 4
NOPE = 128
ROPE = 64
VDIM = 128
QK_DIM = NOPE + ROPE
QK_PAD = 2 * LANES
Q_RANK = 256
KV_RANK = 128
SG_HEADS = 4
SG_DIM = 128
SG_CHUNK = 128
SG_WIDTH = SG_HEADS * SG_DIM
PEER_HEADS = 8
N_KEYS = 128
TOPK = 16
HALF_DIM = 128
N_SEL = PEER_HEADS * TOPK

G_PITCH = N_KEYS + SUBLANES
SC_PITCH = N_KEYS + SUBLANES

_BF = jnp.bfloat16
_F32 = jnp.float32


def _cparams(sem):
    return pltpu.CompilerParams(dimension_semantics=sem, vmem_limit_bytes=VMEM_LIMIT_BYTES)


def _rms(x, g):
    return x * lax.rsqrt(jnp.mean(x * x, axis=-1, keepdims=True) + EPS) * g


def _gelu(x):
    return 0.5 * x * (1.0 + lax.erf(x * (1.0 / math.sqrt(2.0))))


def _dot(a, b):
    return jnp.dot(a, b, preferred_element_type=_F32)


def _dot_nt(a, b):
    return lax.dot_general(a, b, (((1,), (1,)), ((), ())), preferred_element_type=_F32)


def _adaln_kernel(c_ref, w_ref, b_ref, o_ref):
    c = c_ref[...]
    c_act = (c * jax.nn.sigmoid(c)).astype(_BF)
    o_ref[...] = _dot(c_act, w_ref[...].astype(_BF)) + b_ref[...]


def _adaln(c_pad, w_ada, b_ada):
    rows, d = c_pad.shape
    n = w_ada.shape[1]
    tn = 1024
    return pl.pallas_call(
        _adaln_kernel,
        grid=(n // tn,),
        in_specs=[
            pl.BlockSpec((rows, d), lambda j: (0, 0)),
            pl.BlockSpec((d, tn), lambda j: (0, j)),
            pl.BlockSpec((1, tn), lambda j: (0, j)),
        ],
        out_specs=pl.BlockSpec((rows, tn), lambda j: (0, j)),
        out_shape=jax.ShapeDtypeStruct((rows, n), _F32),
        compiler_params=_cparams(("arbitrary",)),
        name="adaln",
    )(c_pad, w_ada, b_ada)


def _inproj_kernel(x_ref, pos_ref, mod_ref, g1_ref, win_ref, gq_ref, wuq_ref, gkv_ref, wukv_ref,
                   invf_ref, gsg_ref, wsg_ref, bsg_ref, gsgo_ref,
                   q_ref, k_ref, v_ref, ysg_ref, *, scale):
    ts = x_ref.shape[1]
    x = x_ref[0]
    shift1 = mod_ref[0, 0:1, :]
    scale1 = mod_ref[0, 1:2, :]
    h = _rms(x, g1_ref[...]) * (1.0 + scale1) + shift1
    z = _dot(h.astype(_BF), win_ref[...])

    o_kv = Q_RANK
    o_kr = o_kv + KV_RANK
    o_krot = o_kr + LANES
    o_u = o_krot + LANES
    o_v = o_u + SG_WIDTH

    ang = pos_ref[0].astype(_F32) * invf_ref[...]
    cos2 = jnp.cos(ang)
    sin2 = jnp.sin(ang)

    cq = _rms(z[:, :Q_RANK], gq_ref[...]).astype(_BF)
    qall = _dot(cq, wuq_ref[...])
    for hd in range(HEADS):
        q_nope = qall[:, hd * NOPE:(hd + 1) * NOPE]
        q_r = qall[:, HEADS * NOPE + hd * LANES: HEADS * NOPE + (hd + 1) * LANES]
        q_rot = qall[:, HEADS * (NOPE + LANES) + hd * LANES: HEADS * (NOPE + LANES) + (hd + 1) * LANES]
        q_ref[0, hd, :, 0:LANES] = (q_nope * scale).astype(_BF)
        q_ref[0, hd, :, LANES:QK_PAD] = ((q_r * cos2 + q_rot * sin2) * scale).astype(_BF)

    ckv = _rms(z[:, o_kv:o_kr], gkv_ref[...]).astype(_BF)
    kv = _dot(ckv, wukv_ref[...])
    k_rope = (z[:, o_kr:o_krot] * cos2 + z[:, o_krot:o_u] * sin2).astype(_BF)
    for hd in range(HEADS):
        base = hd * (NOPE + VDIM)
        k_ref[0, hd, :, 0:LANES] = kv[:, base:base + NOPE].astype(_BF)
        k_ref[0, hd, :, LANES:QK_PAD] = k_rope
        v_ref[0, hd] = kv[:, base + NOPE:base + NOPE + VDIM].astype(_BF)

    u = _gelu(z[:, o_u:o_v])
    vg = _rms(_gelu(z[:, o_v:o_v + SG_WIDTH]), gsg_ref[...]).astype(_BF)
    for ch in range(ts // SG_CHUNK):
        r0 = ch * SG_CHUNK
        parts = []
        for hd in range(SG_HEADS):
            c0 = hd * SG_DIM
            mix = _dot(wsg_ref[hd], vg[r0:r0 + SG_CHUNK, c0:c0 + SG_DIM]) + bsg_ref[hd]
            parts.append(u[r0:r0 + SG_CHUNK, c0:c0 + SG_DIM] * mix)
        y = jnp.concatenate(parts, axis=-1)
        ysg_ref[0, r0:r0 + SG_CHUNK, :] = _rms(y, gsgo_ref[...]).astype(_BF)


def _inproj(x, pos3, mod3, g1, win_ext, gq, wuq_ext, gkv, wukv, invf, gsg, wsg, bsg3, gsgo, ts):
    b, s, d = x.shape
    nz = win_ext.shape[1]
    full = lambda *shape: pl.BlockSpec(shape, lambda bi, i: (0,) * len(shape))
    return pl.pallas_call(
        functools.partial(_inproj_kernel, scale=QK_DIM ** -0.5 * math.log2(math.e)),
        grid=(b, s // ts),
        in_specs=[
            pl.BlockSpec((1, ts, d), lambda bi, i: (bi, i, 0)),
            pl.BlockSpec((1, ts, 1), lambda bi, i: (bi, i, 0)),
            pl.BlockSpec((1, 6, d), lambda bi, i: (bi, 0, 0)),
            full(1, d),
            full(d, nz),
            full(1, Q_RANK),
            full(Q_RANK, wuq_ext.shape[1]),
            full(1, KV_RANK),
            full(KV_RANK, wukv.shape[1]),
            full(1, LANES),
            full(1, SG_WIDTH),
            full(SG_HEADS, SG_CHUNK, SG_CHUNK),
            full(SG_HEADS, SG_CHUNK, 1),
            full(1, SG_WIDTH),
        ],
        out_specs=[
            pl.BlockSpec((1, HEADS, ts, QK_PAD), lambda bi, i: (bi, 0, i, 0)),
            pl.BlockSpec((1, HEADS, ts, QK_PAD), lambda bi, i: (bi, 0, i, 0)),
            pl.BlockSpec((1, HEADS, ts, VDIM), lambda bi, i: (bi, 0, i, 0)),
            pl.BlockSpec((1, ts, SG_WIDTH), lambda bi, i: (bi, i, 0)),
        ],
        out_shape=[
            jax.ShapeDtypeStruct((b, HEADS, s, QK_PAD), _BF),
            jax.ShapeDtypeStruct((b, HEADS, s, QK_PAD), _BF),
            jax.ShapeDtypeStruct((b, HEADS, s, VDIM), _BF),
            jax.ShapeDtypeStruct((b, s, SG_WIDTH), _BF),
        ],
        compiler_params=_cparams(("arbitrary", "arbitrary")),
        name="inproj",
    )(x, pos3, mod3, g1, win_ext, gq, wuq_ext, gkv, wukv, invf, gsg, wsg, bsg3, gsgo)


def _attn_kernel(q_ref, k_ref, v_ref, o_ref, *, tk):
    nk = k_ref.shape[2] // tk
    tq = q_ref.shape[2]
    q = q_ref[0, 0]
    ones = jnp.ones((tk, LANES), _BF)

    def scores(j):
        return _dot_nt(q, k_ref[0, 0, pl.ds(pl.multiple_of(j * tk, tk), tk), :])

    def step(j, m, acc, sc):
        m_new = jnp.maximum(m, jnp.max(sc, axis=-1, keepdims=True))
        alpha = jnp.exp2(m - m_new)
        p = jnp.exp2(sc - m_new).astype(_BF)
        vc = v_ref[0, 0, pl.ds(pl.multiple_of(j * tk, tk), tk), :]
        pv = _dot(p, jnp.concatenate([vc, ones], axis=-1))
        return m_new, alpha * acc + pv

    def body(j, carry):
        m, acc, sc = carry
        sc_next = scores(j + 1)
        m, acc = step(j, m, acc, sc)
        return m, acc, sc_next

    m0 = jnp.full((tq, 1), -jnp.inf, _F32)
    a0 = jnp.zeros((tq, VDIM + LANES), _F32)
    m, acc, sc = lax.fori_loop(0, nk - 1, body, (m0, a0, scores(0)))
    _, acc = step(nk - 1, m, acc, sc)
    o_ref[0] = (acc[:, :VDIM] / acc[:, VDIM:]).astype(_BF)


def _attention(q, k, v, tq, tk):
    b, h, s, _ = q.shape
    return pl.pallas_call(
        functools.partial(_attn_kernel, tk=tk),
        grid=(b, h, s // tq),
        in_specs=[
            pl.BlockSpec((1, 1, tq, QK_PAD), lambda bi, hi, i: (bi, hi, i, 0)),
            pl.BlockSpec((1, 1, s, QK_PAD), lambda bi, hi, i: (bi, hi, 0, 0)),
            pl.BlockSpec((1, 1, s, VDIM), lambda bi, hi, i: (bi, hi, 0, 0)),
        ],
        out_specs=pl.BlockSpec((1, tq, VDIM), lambda bi, hi, i: (bi, i, hi)),
        out_shape=jax.ShapeDtypeStruct((b, s, h * VDIM), _BF),
        compiler_params=_cparams(("arbitrary", "arbitrary", "arbitrary")),
        name="attention",
    )(q, k, v)


def _post_kernel(o_ref, ysg_ref, x_ref, mod_ref, gao_ref, wo_ref, g2_ref, wq_ref, keys_ref,
                 x1_ref, h2_ref, sc_ref):
    gate1 = mod_ref[0, 2:3, :]
    shift2 = mod_ref[0, 3:4, :]
    scale2 = mod_ref[0, 4:5, :]
    width = o_ref.shape[1]
    yn = _rms(o_ref[...].astype(_F32), gao_ref[...]).astype(_BF)
    y = _dot(yn, wo_ref[0:width, :]) + _dot(ysg_ref[...], wo_ref[width:, :])
    x1 = x_ref[...] + gate1 * y
    x1_ref[...] = x1
    h2 = (_rms(x1, g2_ref[...]) * (1.0 + scale2) + shift2).astype(_BF)
    h2_ref[...] = h2
    qp = _dot(h2, wq_ref[...]).astype(_BF)
    pad = jnp.zeros((SC_PITCH - N_KEYS, LANES), _F32)
    for hh in range(2 * PEER_HEADS):
        sc = _dot_nt(keys_ref[hh], qp[:, hh * HALF_DIM:(hh + 1) * HALF_DIM])
        for c in range(sc.shape[1] // LANES):
            sc_ref[hh, c * SC_PITCH:c * SC_PITCH + N_KEYS, :] = sc[:, c * LANES:(c + 1) * LANES]
            sc_ref[hh, c * SC_PITCH + N_KEYS:(c + 1) * SC_PITCH, :] = pad


def _post(o, ysg, x2d, mod3, gao, wo, g2, wq, keys, tt, s):
    t, d = x2d.shape
    width = o.shape[1]
    per_b = s // tt
    full = lambda *shape: pl.BlockSpec(shape, lambda i: (0,) * len(shape))
    return pl.pallas_call(
        _post_kernel,
        grid=(t // tt,),
        in_specs=[
            pl.BlockSpec((tt, width), lambda i: (i, 0)),
            pl.BlockSpec((tt, width), lambda i: (i, 0)),
            pl.BlockSpec((tt, d), lambda i: (i, 0)),
            pl.BlockSpec((1, 6, d), lambda i: (i // per_b, 0, 0)),
            full(1, width),
            full(2 * width, d),
            full(1, d),
            full(d, wq.shape[1]),
            full(2 * PEER_HEADS, N_KEYS, HALF_DIM),
        ],
        out_specs=[
            pl.BlockSpec((tt, d), lambda i: (i, 0)),
            pl.BlockSpec((tt, d), lambda i: (i, 0)),
            pl.BlockSpec((2 * PEER_HEADS, tt // LANES * SC_PITCH, LANES), lambda i: (0, i, 0)),
        ],
        out_shape=[
            jax.ShapeDtypeStruct((t, d), _F32),
            jax.ShapeDtypeStruct((t, d), _BF),
            jax.ShapeDtypeStruct((2 * PEER_HEADS, t // LANES * SC_PITCH, LANES), _F32),
        ],
        compiler_params=_cparams(("arbitrary",)),
        name="post",
    )(o, ysg, x2d, mod3, gao, wo, g2, wq, keys)


_NB = [TOPK // (a + 1) for a in range(TOPK)]


def _oddeven_merge(lo, hi, r):
    step = r * 2
    if step < hi - lo:
        yield from _oddeven_merge(lo, hi, step)
        yield from _oddeven_merge(lo + r, hi, step)
        yield from [(i, i + r) for i in range(lo + r, hi - r, step)]
    else:
        yield (lo, lo + r)


def _oddeven_merge_sort(lo, hi):
    if hi - lo >= 1:
        mid = lo + (hi - lo) // 2
        yield from _oddeven_merge_sort(lo, mid)
        yield from _oddeven_merge_sort(mid + 1, hi)
        yield from _oddeven_merge(lo, hi, 1)


_NET16 = tuple(_oddeven_merge_sort(0, TOPK - 1))

ROUTE_TILES = SUBLANES
ROUTE_TOKENS = ROUTE_TILES * LANES


def _beats(vb, ib, va, ia):
    if isinstance(ia, int) and isinstance(ib, int):
        return vb > va if ib > ia else vb >= va
    return (vb > va) | ((vb == va) & (ib < ia))


def _compare_exchange(v, i, a, b):
    swap = _beats(v[b], i[b], v[a], i[a])
    v[a], v[b] = jnp.where(swap, v[b], v[a]), jnp.where(swap, v[a], v[b])
    i[a], i[b] = jnp.where(swap, i[b], i[a]), jnp.where(swap, i[a], i[b])


def _merge_top16(va, ia, vb, ib):
    cv, ci = [], []
    for r in range(TOPK):
        o = TOPK - 1 - r
        take_b = _beats(vb[o], ib[o], va[r], ia[r])
        cv.append(jnp.where(take_b, vb[o], va[r]))
        ci.append(jnp.where(take_b, ib[o], ia[r]))
    dist = TOPK // 2
    while dist:
        for r in range(TOPK):
            if not r & dist:
                _compare_exchange(cv, ci, r, r + dist)
        dist //= 2
    return cv, ci


def _top16(load_key):
    groups = []
    for g in range(N_KEYS // TOPK):
        v = [load_key(g * TOPK + r) for r in range(TOPK)]
        i = [g * TOPK + r for r in range(TOPK)]
        for a, b in _NET16:
            _compare_exchange(v, i, a, b)
        groups.append((v, i))
    while len(groups) > 1:
        groups = [_merge_top16(*groups[n], *groups[n + 1]) for n in range(0, len(groups), 2)]
    return groups[0]


def _pair_top16(v1, i1, v2, i2):
    cells = [(a, b) for a in range(TOPK) for b in range(_NB[a])]
    cand = [v1[a] + v2[b] for a, b in cells]
    picked = []
    for _ in range(TOPK):
        best = cand[0]
        flat = jnp.zeros(best.shape, jnp.int32)
        for (a, b), val in zip(cells[1:], cand[1:]):
            take = val > best
            best = jnp.where(take, val, best)
            flat = jnp.where(take, a * TOPK + b, flat)
        cand = [jnp.where(flat == a * TOPK + b, -jnp.inf, val) for (a, b), val in zip(cells, cand)]
        a_sel = jnp.right_shift(flat, 4)
        b_sel = jnp.bitwise_and(flat, TOPK - 1)
        row, col = i1[TOPK - 1], i2[TOPK - 1]
        for n in range(TOPK - 2, -1, -1):
            row = jnp.where(a_sel == n, i1[n], row)
            col = jnp.where(b_sel == n, i2[n], col)
        picked.append((best, row, col))
    return picked


def _route_kernel(sc_ref, rows_ref, cols_ref, gate_ref, row_scr, col_scr, gate_scr):
    def one_head(hd, carry):
        def top(hh):
            return _top16(lambda k: sc_ref[hh, pl.ds(k, ROUTE_TILES, stride=SC_PITCH), :])

        v1, i1 = top(2 * hd)
        v2, i2 = top(2 * hd + 1)
        picked = _pair_top16(v1, i1, v2, i2)
        e = [jnp.exp(sc - picked[0][0]) for sc, _, _ in picked]
        z = e[0]
        for ek in e[1:]:
            z = z + ek
        for n, (_, row, col) in enumerate(picked):
            base = pl.multiple_of((hd * TOPK + n) * ROUTE_TILES, ROUTE_TILES)
            row_scr[pl.ds(base, ROUTE_TILES), :] = jnp.broadcast_to(row, z.shape).astype(jnp.int32)
            col_scr[pl.ds(base, ROUTE_TILES), :] = jnp.broadcast_to(col, z.shape).astype(jnp.int32)
            gate_scr[pl.ds(base, ROUTE_TILES), :] = e[n] / z
        return carry

    lax.fori_loop(0, PEER_HEADS, one_head, 0)
    for c in range(ROUTE_TILES):
        tok = slice(c * LANES, (c + 1) * LANES)
        rows_ref[tok, :] = row_scr[pl.ds(c, N_SEL, stride=ROUTE_TILES), :].T
        cols_ref[tok, :] = col_scr[pl.ds(c, N_SEL, stride=ROUTE_TILES), :].T
        gate_ref[tok, :] = gate_scr[pl.ds(c, N_SEL, stride=ROUTE_TILES), :].T


def _route(scores):
    hh, rows, _ = scores.shape
    n_steps = rows // (ROUTE_TILES * SC_PITCH)
    t = n_steps * ROUTE_TOKENS
    return pl.pallas_call(
        _route_kernel,
        grid=(n_steps,),
        in_specs=[pl.BlockSpec((hh, ROUTE_TILES * SC_PITCH, LANES), lambda i: (0, i, 0))],
        out_specs=[pl.BlockSpec((ROUTE_TOKENS, N_SEL), lambda i: (i, 0))] * 3,
        out_shape=[
            jax.ShapeDtypeStruct((t, N_SEL), jnp.int32),
            jax.ShapeDtypeStruct((t, N_SEL), jnp.int32),
            jax.ShapeDtypeStruct((t, N_SEL), _F32),
        ],
        scratch_shapes=[
            pltpu.VMEM((N_SEL * ROUTE_TILES, LANES), jnp.int32),
            pltpu.VMEM((N_SEL * ROUTE_TILES, LANES), jnp.int32),
            pltpu.VMEM((N_SEL * ROUTE_TILES, LANES), _F32),
        ],
        compiler_params=_cparams(("arbitrary",)),
        name="route",
    )(scores)


def _peer_kernel(h2_ref, rows_ref, cols_ref, gate_ref, u_ref, v_ref, x1_ref, mod_ref, gf_ref,
                 o_ref, g_scr, acc_scr):
    tb = h2_ref.shape[0]
    eb = u_ref.shape[0]
    half = tb // 2
    j = pl.program_id(1)

    @pl.when(j == 0)
    def _build_gates():
        acc_scr[...] = jnp.zeros_like(acc_scr)
        key_iota = lax.broadcasted_iota(jnp.int32, (N_KEYS, N_SEL), 0)

        def gate_matrix(t):
            r = rows_ref[pl.ds(t, 1), :]
            cidx = cols_ref[pl.ds(t, 1), :]
            g = gate_ref[pl.ds(t, 1), :]
            a = jnp.where(key_iota == r, g, 0.0).astype(_BF)
            bmat = jnp.where(key_iota == cidx, 1.0, 0.0).astype(_BF)
            return _dot_nt(a, bmat)

        def one(pp, c):
            base = pl.multiple_of(pp * G_PITCH, SUBLANES)
            g_scr[pl.ds(base, N_KEYS), :] = pltpu.pack_elementwise(
                [gate_matrix(pp), gate_matrix(pp + half)], packed_dtype=_BF)
            return c

        lax.fori_loop(0, half, one, 0, unroll=16)

    a = _dot_nt(h2_ref[...], u_ref[...])
    gparts = []
    for ri in range(eb // N_KEYS):
        words = g_scr[pl.ds(j * (eb // N_KEYS) + ri, half, stride=G_PITCH), :]
        gparts.append(jnp.concatenate(
            [pltpu.unpack_elementwise(words, index=k, packed_dtype=_BF, unpacked_dtype=_F32)
             for k in range(2)], axis=0))
    gt = jnp.concatenate(gparts, axis=-1)
    w = (_gelu(a) * gt).astype(_BF)
    acc_scr[...] += _dot(w, v_ref[...])

    @pl.when(j == pl.num_programs(1) - 1)
    def _finish():
        gate2 = mod_ref[0, 5:6, :]
        x2 = x1_ref[...] + gate2 * acc_scr[...]
        o_ref[...] = _rms(x2, gf_ref[...])


def _peer(h2, rows, cols, gates, u_bf, v_bf, x1, mod3, gf, tb, eb, s):
    t, d = h2.shape
    e = u_bf.shape[0]
    per_b = s // tb
    return pl.pallas_call(
        _peer_kernel,
        grid=(t // tb, e // eb),
        in_specs=[
            pl.BlockSpec((tb, d), lambda i, j: (i, 0)),
            pl.BlockSpec((tb, N_SEL), lambda i, j: (i, 0)),
            pl.BlockSpec((tb, N_SEL), lambda i, j: (i, 0)),
            pl.BlockSpec((tb, N_SEL), lambda i, j: (i, 0)),
            pl.BlockSpec((eb, d), lambda i, j: (j, 0)),
            pl.BlockSpec((eb, d), lambda i, j: (j, 0)),
            pl.BlockSpec((tb, d), lambda i, j: (i, 0)),
            pl.BlockSpec((1, 6, d), lambda i, j: (i // per_b, 0, 0)),
            pl.BlockSpec((1, d), lambda i, j: (0, 0)),
        ],
        out_specs=pl.BlockSpec((tb, d), lambda i, j: (i, 0)),
        out_shape=jax.ShapeDtypeStruct((t, d), _F32),
        scratch_shapes=[
            pltpu.VMEM((tb // 2 * G_PITCH, N_KEYS), jnp.uint32),
            pltpu.VMEM((tb, d), _F32),
        ],
        compiler_params=_cparams(("arbitrary", "arbitrary")),
        name="peer",
    )(h2, rows, cols, gates, u_bf, v_bf, x1, mod3, gf)


def _pad_cols(w, width):
    return jnp.pad(w, ((0, 0), (0, width - w.shape[1])))


def _rot_cols(w):
    half = w.shape[1] // 2
    return jnp.concatenate([-w[:, half:], w[:, :half]], axis=1)


def kernel(x, c, positions, w_ada, b_ada, g_norm1, w_in, g_q_a, w_uq, g_kv_a, w_ukv, g_sg, w_sg, b_sg,
           g_attn_out, g_sg_out, w_o, g_norm2, w_peer_q, peer_keys, peer_u, peer_v, g_final):
    b, s, d = x.shape
    depth = w_ada.shape[0]
    assert depth == 1, "the final norm is fused into the last layer's expert kernel"
    n_mod = w_ada.shape[2] // d
    t = b * s

    inv_freq = 1.0 / (ROPE_THETA ** (jnp.arange(0, ROPE, 2, dtype=_F32) / ROPE))
    invf = _pad_cols(jnp.concatenate([inv_freq, inv_freq])[None, :], LANES)
    pos3 = positions[:, :, None]
    c_pad = jnp.pad(c, ((0, SUBLANES - b), (0, 0)))

    o1 = Q_RANK
    o2 = o1 + KV_RANK
    o3 = o2 + ROPE
    xs = x
    for l in range(depth):
        mod = _adaln(c_pad, w_ada[l], b_ada[l][None, :])[:b]
        mod3 = mod.reshape(b, n_mod, d)

        w_kr = w_in[l][:, o2:o3]
        win_ext = jnp.concatenate(
            [w_in[l][:, :o2], _pad_cols(w_kr, LANES), _pad_cols(_rot_cols(w_kr), LANES), w_in[l][:, o3:]],
            axis=1).astype(_BF)
        wq3 = w_uq[l].reshape(Q_RANK, HEADS, QK_DIM)
        wq_nope = wq3[:, :, :NOPE].reshape(Q_RANK, HEADS * NOPE)
        wq_rope = [wq3[:, hd, NOPE:] for hd in range(HEADS)]
        wuq_ext = jnp.concatenate(
            [wq_nope] + [_pad_cols(w, LANES) for w in wq_rope]
            + [_pad_cols(_rot_cols(w), LANES) for w in wq_rope], axis=1).astype(_BF)

        q, k, v, ysg = _inproj(
            xs, pos3, mod3, g_norm1[l][None, :], win_ext, g_q_a[l][None, :], wuq_ext,
            g_kv_a[l][None, :], w_ukv[l].astype(_BF), invf, g_sg[l][None, :], w_sg[l].astype(_BF),
            b_sg[l][:, :, None], g_sg_out[l][None, :], ts=512)
        o = _attention(q, k, v, tq=512, tk=512)

        keys = peer_keys[l].reshape(2 * PEER_HEADS, N_KEYS, HALF_DIM).astype(_BF)
        x1, h2, scores = _post(
            o.reshape(t, HEADS * VDIM), ysg.reshape(t, SG_WIDTH), xs.reshape(t, d), mod3,
            g_attn_out[l][None, :], w_o[l].astype(_BF), g_norm2[l][None, :], w_peer_q[l].astype(_BF),
            keys, tt=256, s=s)
        rows, cols, gates = _route(scores)
        out = _peer(h2, rows, cols, gates, peer_u[l].astype(_BF), peer_v[l].astype(_BF), x1, mod3,
                    g_final[None, :], tb=512, eb=2048, s=s)
        xs = out.reshape(b, s, d)
    return xs
```

```python
import functools
import math

import jax
import jax.numpy as jnp
from jax import lax
from jax.experimental import pallas as pl
from jax.experimental.pallas import tpu as pltpu

LANES = 128
SUBLANES = 8
VMEM_LIMIT_BYTES = 56 * 1024 * 1024

EPS = 1e-6
ROPE_THETA = 10000.0
HEADS = 4
NOPE = 128
ROPE = 64
VDIM = 128
QK_DIM = NOPE + ROPE
QK_PAD = 2 * LANES
Q_RANK = 256
KV_RANK = 128
SG_HEADS = 4
SG_DIM = 128
SG_CHUNK = 128
SG_WIDTH = SG_HEADS * SG_DIM
PEER_HEADS = 8
N_KEYS = 128
TOPK = 16
HALF_DIM = 128
N_SEL = PEER_HEADS * TOPK

G_PITCH = N_KEYS + SUBLANES
SC_PITCH = N_KEYS + SUBLANES

_BF = jnp.bfloat16
_F32 = jnp.float32


def _cparams(sem):
    return pltpu.CompilerParams(dimension_semantics=sem, vmem_limit_bytes=VMEM_LIMIT_BYTES)


def _rms(x, g):
    return x * lax.rsqrt(jnp.mean(x * x, axis=-1, keepdims=True) + EPS) * g


def _gelu(x):
    return 0.5 * x * (1.0 + lax.erf(x * (1.0 / math.sqrt(2.0))))


def _dot(a, b):
    return jnp.dot(a, b, preferred_element_type=_F32)


def _dot_nt(a, b):
    return lax.dot_general(a, b, (((1,), (1,)), ((), ())), preferred_element_type=_F32)


def _adaln_kernel(c_ref, w_ref, b_ref, o_ref):
    c = c_ref[...]
    c_act = (c * jax.nn.sigmoid(c)).astype(_BF)
    o_ref[...] = _dot(c_act, w_ref[...].astype(_BF)) + b_ref[...]


def _adaln(c_pad, w_ada, b_ada):
    rows, d = c_pad.shape
    n = w_ada.shape[1]
    tn = 1024
    return pl.pallas_call(
        _adaln_kernel,
        grid=(n // tn,),
        in_specs=[
            pl.BlockSpec((rows, d), lambda j: (0, 0)),
            pl.BlockSpec((d, tn), lambda j: (0, j)),
            pl.BlockSpec((1, tn), lambda j: (0, j)),
        ],
        out_specs=pl.BlockSpec((rows, tn), lambda j: (0, j)),
        out_shape=jax.ShapeDtypeStruct((rows, n), _F32),
        compiler_params=_cparams(("arbitrary",)),
        name="adaln",
    )(c_pad, w_ada, b_ada)


def _inproj_kernel(x_ref, pos_ref, mod_ref, g1_ref, win_ref, gq_ref, wuq_ref, gkv_ref, wukv_ref,
                   invf_ref, gsg_ref, wsg_ref, bsg_ref, gsgo_ref,
                   q_ref, k_ref, v_ref, ysg_ref, *, scale):
    ts = x_ref.shape[1]
    x = x_ref[0]
    shift1 = mod_ref[0, 0:1, :]
    scale1 = mod_ref[0, 1:2, :]
    h = _rms(x, g1_ref[...]) * (1.0 + scale1) + shift1
    z = _dot(h.astype(_BF), win_ref[...])

    o_kv = Q_RANK
    o_kr = o_kv + KV_RANK
    o_krot = o_kr + LANES
    o_u = o_krot + LANES
    o_v = o_u + SG_WIDTH

    ang = pos_ref[0].astype(_F32) * invf_ref[...]
    cos2 = jnp.cos(ang)
    sin2 = jnp.sin(ang)

    cq = _rms(z[:, :Q_RANK], gq_ref[...]).astype(_BF)
    qall = _dot(cq, wuq_ref[...])
    for hd in range(HEADS):
        q_nope = qall[:, hd * NOPE:(hd + 1) * NOPE]
        q_r = qall[:, HEADS * NOPE + hd * LANES: HEADS * NOPE + (hd + 1) * LANES]
        q_rot = qall[:, HEADS * (NOPE + LANES) + hd * LANES: HEADS * (NOPE + LANES) + (hd + 1) * LANES]
        q_ref[0, hd, :, 0:LANES] = (q_nope * scale).astype(_BF)
        q_ref[0, hd, :, LANES:QK_PAD] = ((q_r * cos2 + q_rot * sin2) * scale).astype(_BF)

    ckv = _rms(z[:, o_kv:o_kr], gkv_ref[...]).astype(_BF)
    kv = _dot(ckv, wukv_ref[...])
    k_rope = (z[:, o_kr:o_krot] * cos2 + z[:, o_krot:o_u] * sin2).astype(_BF)
    for hd in range(HEADS):
        base = hd * (NOPE + VDIM)
        k_ref[0, hd, :, 0:LANES] = kv[:, base:base + NOPE].astype(_BF)
        k_ref[0, hd, :, LANES:QK_PAD] = k_rope
        v_ref[0, hd] = kv[:, base + NOPE:base + NOPE + VDIM].astype(_BF)

    u = _gelu(z[:, o_u:o_v])
    vg = _rms(_gelu(z[:, o_v:o_v + SG_WIDTH]), gsg_ref[...]).astype(_BF)
    for ch in range(ts // SG_CHUNK):
        r0 = ch * SG_CHUNK
        parts = []
        for hd in range(SG_HEADS):
            c0 = hd * SG_DIM
            mix = _dot(wsg_ref[hd], vg[r0:r0 + SG_CHUNK, c0:c0 + SG_DIM]) + bsg_ref[hd]
            parts.append(u[r0:r0 + SG_CHUNK, c0:c0 + SG_DIM] * mix)
        y = jnp.concatenate(parts, axis=-1)
        ysg_ref[0, r0:r0 + SG_CHUNK, :] = _rms(y, gsgo_ref[...]).astype(_BF)


def _inproj(x, pos3, mod3, g1, win_ext, gq, wuq_ext, gkv, wukv, invf, gsg, wsg, bsg3, gsgo, ts):
    b, s, d = x.shape
    nz = win_ext.shape[1]
    full = lambda *shape: pl.BlockSpec(shape, lambda bi, i: (0,) * len(shape))
    return pl.pallas_call(
        functools.partial(_inproj_kernel, scale=QK_DIM ** -0.5 * math.log2(math.e)),
        grid=(b, s // ts),
        in_specs=[
            pl.BlockSpec((1, ts, d), lambda bi, i: (bi, i, 0)),
            pl.BlockSpec((1, ts, 1), lambda bi, i: (bi, i, 0)),
            pl.BlockSpec((1, 6, d), lambda bi, i: (bi, 0, 0)),
            full(1, d),
            full(d, nz),
            full(1, Q_RANK),
            full(Q_RANK, wuq_ext.shape[1]),
            full(1, KV_RANK),
            full(KV_RANK, wukv.shape[1]),
            full(1, LANES),
            full(1, SG_WIDTH),
            full(SG_HEADS, SG_CHUNK, SG_CHUNK),
            full(SG_HEADS, SG_CHUNK, 1),
            full(1, SG_WIDTH),
        ],
        out_specs=[
            pl.BlockSpec((1, HEADS, ts, QK_PAD), lambda bi, i: (bi, 0, i, 0)),
            pl.BlockSpec((1, HEADS, ts, QK_PAD), lambda bi, i: (bi, 0, i, 0)),
            pl.BlockSpec((1, HEADS, ts, VDIM), lambda bi, i: (bi, 0, i, 0)),
            pl.BlockSpec((1, ts, SG_WIDTH), lambda bi, i: (bi, i, 0)),
        ],
        out_shape=[
            jax.ShapeDtypeStruct((b, HEADS, s, QK_PAD), _BF),
            jax.ShapeDtypeStruct((b, HEADS, s, QK_PAD), _BF),
            jax.ShapeDtypeStruct((b, HEADS, s, VDIM), _BF),
            jax.ShapeDtypeStruct((b, s, SG_WIDTH), _BF),
        ],
        compiler_params=_cparams(("arbitrary", "arbitrary")),
        name="inproj",
    )(x, pos3, mod3, g1, win_ext, gq, wuq_ext, gkv, wukv, invf, gsg, wsg, bsg3, gsgo)


def _attn_kernel(q_ref, k_ref, v_ref, o_ref, s_scr, *, unroll):
    nk, tq, tk = s_scr.shape
    q = q_ref[0, 0]
    ones = jnp.ones((tk, LANES), _BF)

    def score_chunk(j, mrun):
        sc = _dot_nt(q, k_ref[0, 0, pl.ds(pl.multiple_of(j * tk, tk), tk), :])
        s_scr[j] = sc
        for c in range(tk // LANES):
            mrun = jnp.maximum(mrun, sc[:, c * LANES:(c + 1) * LANES])
        return mrun

    mrun = lax.fori_loop(0, nk, score_chunk, jnp.full((tq, LANES), -jnp.inf, _F32), unroll=unroll)
    m = jnp.max(mrun, axis=-1, keepdims=True)

    def value_chunk(j, acc):
        p = jnp.exp2(s_scr[j] - m).astype(_BF)
        vc = v_ref[0, 0, pl.ds(pl.multiple_of(j * tk, tk), tk), :]
        return acc + _dot(p, jnp.concatenate([vc, ones], axis=-1))

    acc = lax.fori_loop(0, nk, value_chunk, jnp.zeros((tq, VDIM + LANES), _F32), unroll=unroll)
    o_ref[0] = (acc[:, :VDIM] / acc[:, VDIM:]).astype(_BF)


def _attention(q, k, v, tq, tk):
    b, h, s, _ = q.shape
    return pl.pallas_call(
        functools.partial(_attn_kernel, unroll=8),
        grid=(b, h, s // tq),
        scratch_shapes=[pltpu.VMEM((s // tk, tq, tk), _F32)],
        in_specs=[
            pl.BlockSpec((1, 1, tq, QK_PAD), lambda bi, hi, i: (bi, hi, i, 0)),
            pl.BlockSpec((1, 1, s, QK_PAD), lambda bi, hi, i: (bi, hi, 0, 0)),
            pl.BlockSpec((1, 1, s, VDIM), lambda bi, hi, i: (bi, hi, 0, 0)),
        ],
        out_specs=pl.BlockSpec((1, tq, VDIM), lambda bi, hi, i: (bi, i, hi)),
        out_shape=jax.ShapeDtypeStruct((b, s, h * VDIM), _BF),
        compiler_params=_cparams(("arbitrary", "arbitrary", "arbitrary")),
        name="attention",
    )(q, k, v)


def _post_kernel(o_ref, ysg_ref, x_ref, mod_ref, gao_ref, wo_ref, g2_ref, wq_ref, keys_ref,
                 x1_ref, h2_ref, sc_ref):
    gate1 = mod_ref[0, 2:3, :]
    shift2 = mod_ref[0, 3:4, :]
    scale2 = mod_ref[0, 4:5, :]
    width = o_ref.shape[1]
    yn = _rms(o_ref[...].astype(_F32), gao_ref[...]).astype(_BF)
    y = _dot(yn, wo_ref[0:width, :]) + _dot(ysg_ref[...], wo_ref[width:, :])
    x1 = x_ref[...] + gate1 * y
    x1_ref[...] = x1
    h2 = (_rms(x1, g2_ref[...]) * (1.0 + scale2) + shift2).astype(_BF)
    h2_ref[...] = h2
    qp = _dot(h2, wq_ref[...]).astype(_BF)
    pad = jnp.zeros((SC_PITCH - N_KEYS, LANES), _F32)
    for hh in range(2 * PEER_HEADS):
        sc = _dot_nt(keys_ref[hh], qp[:, hh * HALF_DIM:(hh + 1) * HALF_DIM])
        for c in range(sc.shape[1] // LANES):
            sc_ref[hh, c * SC_PITCH:c * SC_PITCH + N_KEYS, :] = sc[:, c * LANES:(c + 1) * LANES]
            sc_ref[hh, c * SC_PITCH + N_KEYS:(c + 1) * SC_PITCH, :] = pad


def _post(o, ysg, x2d, mod3, gao, wo, g2, wq, keys, tt, s):
    t, d = x2d.shape
    width = o.shape[1]
    per_b = s // tt
    full = lambda *shape: pl.BlockSpec(shape, lambda i: (0,) * len(shape))
    return pl.pallas_call(
        _post_kernel,
        grid=(t // tt,),
        in_specs=[
            pl.BlockSpec((tt, width), lambda i: (i, 0)),
            pl.BlockSpec((tt, width), lambda i: (i, 0)),
            pl.BlockSpec((tt, d), lambda i: (i, 0)),
            pl.BlockSpec((1, 6, d), lambda i: (i // per_b, 0, 0)),
            full(1, width),
            full(2 * width, d),
            full(1, d),
            full(d, wq.shape[1]),
            full(2 * PEER_HEADS, N_KEYS, HALF_DIM),
        ],
        out_specs=[
            pl.BlockSpec((tt, d), lambda i: (i, 0)),
            pl.BlockSpec((tt, d), lambda i: (i, 0)),
            pl.BlockSpec((2 * PEER_HEADS, tt // LANES * SC_PITCH, LANES), lambda i: (0, i, 0)),
        ],
        out_shape=[
            jax.ShapeDtypeStruct((t, d), _F32),
            jax.ShapeDtypeStruct((t, d), _BF),
            jax.ShapeDtypeStruct((2 * PEER_HEADS, t // LANES * SC_PITCH, LANES), _F32),
        ],
        compiler_params=_cparams(("arbitrary",)),
        name="post",
    )(o, ysg, x2d, mod3, gao, wo, g2, wq, keys)


_NB = [TOPK // (a + 1) for a in range(TOPK)]


def _oddeven_merge(lo, hi, r):
    step = r * 2
    if step < hi - lo:
        yield from _oddeven_merge(lo, hi, step)
        yield from _oddeven_merge(lo + r, hi, step)
        yield from [(i, i + r) for i in range(lo + r, hi - r, step)]
    else:
        yield (lo, lo + r)


def _oddeven_merge_sort(lo, hi):
    if hi - lo >= 1:
        mid = lo + (hi - lo) // 2
        yield from _oddeven_merge_sort(lo, mid)
        yield from _oddeven_merge_sort(mid + 1, hi)
        yield from _oddeven_merge(lo, hi, 1)


_NET16 = tuple(_oddeven_merge_sort(0, TOPK - 1))

ROUTE_TILES = SUBLANES
ROUTE_TOKENS = ROUTE_TILES * LANES


def _beats(vb, ib, va, ia):
    if isinstance(ia, int) and isinstance(ib, int):
        return vb > va if ib > ia else vb >= va
    return (vb > va) | ((vb == va) & (ib < ia))


def _compare_exchange(v, i, a, b):
    swap = _beats(v[b], i[b], v[a], i[a])
    v[a], v[b] = jnp.where(swap, v[b], v[a]), jnp.where(swap, v[a], v[b])
    i[a], i[b] = jnp.where(swap, i[b], i[a]), jnp.where(swap, i[a], i[b])


def _merge_top16(va, ia, vb, ib):
    cv, ci = [], []
    for r in range(TOPK):
        o = TOPK - 1 - r
        take_b = _beats(vb[o], ib[o], va[r], ia[r])
        cv.append(jnp.where(take_b, vb[o], va[r]))
        ci.append(jnp.where(take_b, ib[o], ia[r]))
    dist = TOPK // 2
    while dist:
        for r in range(TOPK):
            if not r & dist:
                _compare_exchange(cv, ci, r, r + dist)
        dist //= 2
    return cv, ci


def _top16(load_key):
    groups = []
    for g in range(N_KEYS // TOPK):
        v = [load_key(g * TOPK + r) for r in range(TOPK)]
        i = [g * TOPK + r for r in range(TOPK)]
        for a, b in _NET16:
            _compare_exchange(v, i, a, b)
        groups.append((v, i))
    while len(groups) > 1:
        groups = [_merge_top16(*groups[n], *groups[n + 1]) for n in range(0, len(groups), 2)]
    return groups[0]


def _pair_top16(v1, i1, v2, i2):
    cells = [(a, b) for a in range(TOPK) for b in range(_NB[a])]
    cand = [v1[a] + v2[b] for a, b in cells]
    picked = []
    for _ in range(TOPK):
        best = cand[0]
        flat = jnp.zeros(best.shape, jnp.int32)
        for (a, b), val in zip(cells[1:], cand[1:]):
            take = val > best
            best = jnp.where(take, val, best)
            flat = jnp.where(take, a * TOPK + b, flat)
        cand = [jnp.where(flat == a * TOPK + b, -jnp.inf, val) for (a, b), val in zip(cells, cand)]
        a_sel = jnp.right_shift(flat, 4)
        b_sel = jnp.bitwise_and(flat, TOPK - 1)
        row, col = i1[TOPK - 1], i2[TOPK - 1]
        for n in range(TOPK - 2, -1, -1):
            row = jnp.where(a_sel == n, i1[n], row)
            col = jnp.where(b_sel == n, i2[n], col)
        picked.append((best, row, col))
    return picked


def _route_kernel(sc_ref, rows_ref, cols_ref, gate_ref, row_scr, col_scr, gate_scr):
    def one_head(hd, carry):
        def top(hh):
            return _top16(lambda k: sc_ref[hh, pl.ds(k, ROUTE_TILES, stride=SC_PITCH), :])

        v1, i1 = top(2 * hd)
        v2, i2 = top(2 * hd + 1)
        picked = _pair_top16(v1, i1, v2, i2)
        e = [jnp.exp(sc - picked[0][0]) for sc, _, _ in picked]
        z = e[0]
        for ek in e[1:]:
            z = z + ek
        for n, (_, row, col) in enumerate(picked):
            base = pl.multiple_of((hd * TOPK + n) * ROUTE_TILES, ROUTE_TILES)
            row_scr[pl.ds(base, ROUTE_TILES), :] = jnp.broadcast_to(row, z.shape).astype(jnp.int32)
            col_scr[pl.ds(base, ROUTE_TILES), :] = jnp.broadcast_to(col, z.shape).astype(jnp.int32)
            gate_scr[pl.ds(base, ROUTE_TILES), :] = e[n] / z
        return carry

    lax.fori_loop(0, PEER_HEADS, one_head, 0)
    for c in range(ROUTE_TILES):
        tok = slice(c * LANES, (c + 1) * LANES)
        rows_ref[tok, :] = row_scr[pl.ds(c, N_SEL, stride=ROUTE_TILES), :].T
        cols_ref[tok, :] = col_scr[pl.ds(c, N_SEL, stride=ROUTE_TILES), :].T
        gate_ref[tok, :] = gate_scr[pl.ds(c, N_SEL, stride=ROUTE_TILES), :].T


def _route(scores):
    hh, rows, _ = scores.shape
    n_steps = rows // (ROUTE_TILES * SC_PITCH)
    t = n_steps * ROUTE_TOKENS
    return pl.pallas_call(
        _route_kernel,
        grid=(n_steps,),
        in_specs=[pl.BlockSpec((hh, ROUTE_TILES * SC_PITCH, LANES), lambda i: (0, i, 0))],
        out_specs=[pl.BlockSpec((ROUTE_TOKENS, N_SEL), lambda i: (i, 0))] * 3,
        out_shape=[
            jax.ShapeDtypeStruct((t, N_SEL), jnp.int32),
            jax.ShapeDtypeStruct((t, N_SEL), jnp.int32),
            jax.ShapeDtypeStruct((t, N_SEL), _F32),
        ],
        scratch_shapes=[
            pltpu.VMEM((N_SEL * ROUTE_TILES, LANES), jnp.int32),
            pltpu.VMEM((N_SEL * ROUTE_TILES, LANES), jnp.int32),
            pltpu.VMEM((N_SEL * ROUTE_TILES, LANES), _F32),
        ],
        compiler_params=_cparams(("arbitrary",)),
        name="route",
    )(scores)


def _peer_kernel(h2_ref, rows_ref, cols_ref, gate_ref, u_ref, v_ref, x1_ref, mod_ref, gf_ref,
                 o_ref, g_scr, acc_scr):
    tb = h2_ref.shape[0]
    eb = u_ref.shape[0]
    half = tb // 2
    j = pl.program_id(1)

    @pl.when(j == 0)
    def _build_gates():
        acc_scr[...] = jnp.zeros_like(acc_scr)
        key_iota = lax.broadcasted_iota(jnp.int32, (N_KEYS, N_SEL), 0)

        def gate_matrix(t):
            r = rows_ref[pl.ds(t, 1), :]
            cidx = cols_ref[pl.ds(t, 1), :]
            g = gate_ref[pl.ds(t, 1), :]
            a = jnp.where(key_iota == r, g, 0.0).astype(_BF)
            bmat = jnp.where(key_iota == cidx, 1.0, 0.0).astype(_BF)
            return _dot_nt(a, bmat)

        def one(pp, c):
            base = pl.multiple_of(pp * G_PITCH, SUBLANES)
            g_scr[pl.ds(base, N_KEYS), :] = pltpu.pack_elementwise(
                [gate_matrix(pp), gate_matrix(pp + half)], packed_dtype=_BF)
            return c

        lax.fori_loop(0, half, one, 0, unroll=16)

    a = _dot_nt(h2_ref[...], u_ref[...])
    gparts = []
    for ri in range(eb // N_KEYS):
        words = g_scr[pl.ds(j * (eb // N_KEYS) + ri, half, stride=G_PITCH), :]
        gparts.append(jnp.concatenate(
            [pltpu.unpack_elementwise(words, index=k, packed_dtype=_BF, unpacked_dtype=_F32)
             for k in range(2)], axis=0))
    gt = jnp.concatenate(gparts, axis=-1)
    w = (_gelu(a) * gt).astype(_BF)
    acc_scr[...] += _dot(w, v_ref[...])

    @pl.when(j == pl.num_programs(1) - 1)
    def _finish():
        gate2 = mod_ref[0, 5:6, :]
        x2 = x1_ref[...] + gate2 * acc_scr[...]
        o_ref[...] = _rms(x2, gf_ref[...])


def _peer(h2, rows, cols, gates, u_bf, v_bf, x1, mod3, gf, tb, eb, s):
    t, d = h2.shape
    e = u_bf.shape[0]
    per_b = s // tb
    return pl.pallas_call(
        _peer_kernel,
        grid=(t // tb, e // eb),
        in_specs=[
            pl.BlockSpec((tb, d), lambda i, j: (i, 0)),
            pl.BlockSpec((tb, N_SEL), lambda i, j: (i, 0)),
            pl.BlockSpec((tb, N_SEL), lambda i, j: (i, 0)),
            pl.BlockSpec((tb, N_SEL), lambda i, j: (i, 0)),
            pl.BlockSpec((eb, d), lambda i, j: (j, 0)),
            pl.BlockSpec((eb, d), lambda i, j: (j, 0)),
            pl.BlockSpec((tb, d), lambda i, j: (i, 0)),
            pl.BlockSpec((1, 6, d), lambda i, j: (i // per_b, 0, 0)),
            pl.BlockSpec((1, d), lambda i, j: (0, 0)),
        ],
        out_specs=pl.BlockSpec((tb, d), lambda i, j: (i, 0)),
        out_shape=jax.ShapeDtypeStruct((t, d), _F32),
        scratch_shapes=[
            pltpu.VMEM((tb // 2 * G_PITCH, N_KEYS), jnp.uint32),
            pltpu.VMEM((tb, d), _F32),
        ],
        compiler_params=_cparams(("arbitrary", "arbitrary")),
        name="peer",
    )(h2, rows, cols, gates, u_bf, v_bf, x1, mod3, gf)


def _pad_cols(w, width):
    return jnp.pad(w, ((0, 0), (0, width - w.shape[1])))


def _rot_cols(w):
    half = w.shape[1] // 2
    return jnp.concatenate([-w[:, half:], w[:, :half]], axis=1)


def kernel(x, c, positions, w_ada, b_ada, g_norm1, w_in, g_q_a, w_uq, g_kv_a, w_ukv, g_sg, w_sg, b_sg,
           g_attn_out, g_sg_out, w_o, g_norm2, w_peer_q, peer_keys, peer_u, peer_v, g_final):
    b, s, d = x.shape
    depth = w_ada.shape[0]
    assert depth == 1, "the final norm is fused into the last layer's expert kernel"
    n_mod = w_ada.shape[2] // d
    t = b * s

    inv_freq = 1.0 / (ROPE_THETA ** (jnp.arange(0, ROPE, 2, dtype=_F32) / ROPE))
    invf = _pad_cols(jnp.concatenate([inv_freq, inv_freq])[None, :], LANES)
    pos3 = positions[:, :, None]
    c_pad = jnp.pad(c, ((0, SUBLANES - b), (0, 0)))

    o1 = Q_RANK
    o2 = o1 + KV_RANK
    o3 = o2 + ROPE
    xs = x
    for l in range(depth):
        mod = _adaln(c_pad, w_ada[l], b_ada[l][None, :])[:b]
        mod3 = mod.reshape(b, n_mod, d)

        w_kr = w_in[l][:, o2:o3]
        win_ext = jnp.concatenate(
            [w_in[l][:, :o2], _pad_cols(w_kr, LANES), _pad_cols(_rot_cols(w_kr), LANES), w_in[l][:, o3:]],
            axis=1).astype(_BF)
        wq3 = w_uq[l].reshape(Q_RANK, HEADS, QK_DIM)
        wq_nope = wq3[:, :, :NOPE].reshape(Q_RANK, HEADS * NOPE)
        wq_rope = [wq3[:, hd, NOPE:] for hd in range(HEADS)]
        wuq_ext = jnp.concatenate(
            [wq_nope] + [_pad_cols(w, LANES) for w in wq_rope]
            + [_pad_cols(_rot_cols(w), LANES) for w in wq_rope], axis=1).astype(_BF)

        q, k, v, ysg = _inproj(
            xs, pos3, mod3, g_norm1[l][None, :], win_ext, g_q_a[l][None, :], wuq_ext,
            g_kv_a[l][None, :], w_ukv[l].astype(_BF), invf, g_sg[l][None, :], w_sg[l].astype(_BF),
            b_sg[l][:, :, None], g_sg_out[l][None, :], ts=512)
        o = _attention(q, k, v, tq=1024, tk=256)

        keys = peer_keys[l].reshape(2 * PEER_HEADS, N_KEYS, HALF_DIM).astype(_BF)
        x1, h2, scores = _post(
            o.reshape(t, HEADS * VDIM), ysg.reshape(t, SG_WIDTH), xs.reshape(t, d), mod3,
            g_attn_out[l][None, :], w_o[l].astype(_BF), g_norm2[l][None, :], w_peer_q[l].astype(_BF),
            keys, tt=256, s=s)
        rows, cols, gates = _route(scores)
        out = _peer(h2, rows, cols, gates, peer_u[l].astype(_BF), peer_v[l].astype(_BF), x1, mod3,
                    g_final[None, :], tb=512, eb=2048, s=s)
        xs = out.reshape(b, s, d)
    return xs
```

```python
import functools
import math

import jax
import jax.numpy as jnp
from jax import lax
from jax.experimental import pallas as pl
from jax.experimental.pallas import tpu as pltpu

LANES = 128
SUBLANES = 8
VMEM_LIMIT_BYTES = 56 * 1024 * 1024

EPS = 1e-6
ROPE_THETA = 10000.0
HEADS = 4
NOPE = 128
ROPE = 64
VDIM = 128
QK_DIM = NOPE + ROPE
QK_PAD = 2 * LANES
Q_RANK = 256
KV_RANK = 128
SG_HEADS = 4
SG_DIM = 128
SG_CHUNK = 128
SG_WIDTH = SG_HEADS * SG_DIM
PEER_HEADS = 8
N_KEYS = 128
TOPK = 16
HALF_DIM = 128
N_SEL = PEER_HEADS * TOPK

G_PITCH = N_KEYS + SUBLANES
SC_PITCH = N_KEYS + SUBLANES

_BF = jnp.bfloat16
_F32 = jnp.float32


def _cparams(sem):
    return pltpu.CompilerParams(dimension_semantics=sem, vmem_limit_bytes=VMEM_LIMIT_BYTES)


def _rms(x, g):
    return x * lax.rsqrt(jnp.mean(x * x, axis=-1, keepdims=True) + EPS) * g


def _gelu(x):
    return 0.5 * x * (1.0 + lax.erf(x * (1.0 / math.sqrt(2.0))))


def _dot(a, b):
    return jnp.dot(a, b, preferred_element_type=_F32)


def _dot_nt(a, b):
    return lax.dot_general(a, b, (((1,), (1,)), ((), ())), preferred_element_type=_F32)


def _adaln_kernel(c_ref, w_ref, b_ref, o_ref):
    c = c_ref[...]
    c_act = (c * jax.nn.sigmoid(c)).astype(_BF)
    o_ref[...] = _dot(c_act, w_ref[...].astype(_BF)) + b_ref[...]


def _adaln(c_pad, w_ada, b_ada):
    rows, d = c_pad.shape
    n = w_ada.shape[1]
    tn = 1024
    return pl.pallas_call(
        _adaln_kernel,
        grid=(n // tn,),
        in_specs=[
            pl.BlockSpec((rows, d), lambda j: (0, 0)),
            pl.BlockSpec((d, tn), lambda j: (0, j)),
            pl.BlockSpec((1, tn), lambda j: (0, j)),
        ],
        out_specs=pl.BlockSpec((rows, tn), lambda j: (0, j)),
        out_shape=jax.ShapeDtypeStruct((rows, n), _F32),
        compiler_params=_cparams(("arbitrary",)),
        name="adaln",
    )(c_pad, w_ada, b_ada)


def _inproj_kernel(x_ref, pos_ref, mod_ref, g1_ref, win_ref, gq_ref, wuq_ref, gkv_ref, wukv_ref,
                   invf_ref, gsg_ref, wsg_ref, bsg_ref, gsgo_ref,
                   q_ref, k_ref, v_ref, ysg_ref, *, scale):
    ts = x_ref.shape[1]
    x = x_ref[0]
    shift1 = mod_ref[0, 0:1, :]
    scale1 = mod_ref[0, 1:2, :]
    h = _rms(x, g1_ref[...]) * (1.0 + scale1) + shift1
    z = _dot(h.astype(_BF), win_ref[...])

    o_kv = Q_RANK
    o_kr = o_kv + KV_RANK
    o_krot = o_kr + LANES
    o_u = o_krot + LANES
    o_v = o_u + SG_WIDTH

    ang = pos_ref[0].astype(_F32) * invf_ref[...]
    cos2 = jnp.cos(ang)
    sin2 = jnp.sin(ang)

    cq = _rms(z[:, :Q_RANK], gq_ref[...]).astype(_BF)
    qall = _dot(cq, wuq_ref[...])
    for hd in range(HEADS):
        q_nope = qall[:, hd * NOPE:(hd + 1) * NOPE]
        q_r = qall[:, HEADS * NOPE + hd * LANES: HEADS * NOPE + (hd + 1) * LANES]
        q_rot = qall[:, HEADS * (NOPE + LANES) + hd * LANES: HEADS * (NOPE + LANES) + (hd + 1) * LANES]
        q_ref[0, hd, :, 0:LANES] = (q_nope * scale).astype(_BF)
        q_ref[0, hd, :, LANES:QK_PAD] = ((q_r * cos2 + q_rot * sin2) * scale).astype(_BF)

    ckv = _rms(z[:, o_kv:o_kr], gkv_ref[...]).astype(_BF)
    kv = _dot(ckv, wukv_ref[...])
    k_rope = (z[:, o_kr:o_krot] * cos2 + z[:, o_krot:o_u] * sin2).astype(_BF)
    for hd in range(HEADS):
        base = hd * (NOPE + VDIM)
        k_ref[0, hd, :, 0:LANES] = kv[:, base:base + NOPE].astype(_BF)
        k_ref[0, hd, :, LANES:QK_PAD] = k_rope
        v_ref[0, hd] = kv[:, base + NOPE:base + NOPE + VDIM].astype(_BF)

    u = _gelu(z[:, o_u:o_v])
    vg = _rms(_gelu(z[:, o_v:o_v + SG_WIDTH]), gsg_ref[...]).astype(_BF)
    for ch in range(ts // SG_CHUNK):
        r0 = ch * SG_CHUNK
        parts = []
        for hd in range(SG_HEADS):
            c0 = hd * SG_DIM
            mix = _dot(wsg_ref[hd], vg[r0:r0 + SG_CHUNK, c0:c0 + SG_DIM]) + bsg_ref[hd]
            parts.append(u[r0:r0 + SG_CHUNK, c0:c0 + SG_DIM] * mix)
        y = jnp.concatenate(parts, axis=-1)
        ysg_ref[0, r0:r0 + SG_CHUNK, :] = _rms(y, gsgo_ref[...]).astype(_BF)


def _inproj(x, pos3, mod3, g1, win_ext, gq, wuq_ext, gkv, wukv, invf, gsg, wsg, bsg3, gsgo, ts):
    b, s, d = x.shape
    nz = win_ext.shape[1]
    full = lambda *shape: pl.BlockSpec(shape, lambda bi, i: (0,) * len(shape))
    return pl.pallas_call(
        functools.partial(_inproj_kernel, scale=QK_DIM ** -0.5 * math.log2(math.e)),
        grid=(b, s // ts),
        in_specs=[
            pl.BlockSpec((1, ts, d), lambda bi, i: (bi, i, 0)),
            pl.BlockSpec((1, ts, 1), lambda bi, i: (bi, i, 0)),
            pl.BlockSpec((1, 6, d), lambda bi, i: (bi, 0, 0)),
            full(1, d),
            full(d, nz),
            full(1, Q_RANK),
            full(Q_RANK, wuq_ext.shape[1]),
            full(1, KV_RANK),
            full(KV_RANK, wukv.shape[1]),
            full(1, LANES),
            full(1, SG_WIDTH),
            full(SG_HEADS, SG_CHUNK, SG_CHUNK),
            full(SG_HEADS, SG_CHUNK, 1),
            full(1, SG_WIDTH),
        ],
        out_specs=[
            pl.BlockSpec((1, HEADS, ts, QK_PAD), lambda bi, i: (bi, 0, i, 0)),
            pl.BlockSpec((1, HEADS, ts, QK_PAD), lambda bi, i: (bi, 0, i, 0)),
            pl.BlockSpec((1, HEADS, ts, VDIM), lambda bi, i: (bi, 0, i, 0)),
            pl.BlockSpec((1, ts, SG_WIDTH), lambda bi, i: (bi, i, 0)),
        ],
        out_shape=[
            jax.ShapeDtypeStruct((b, HEADS, s, QK_PAD), _BF),
            jax.ShapeDtypeStruct((b, HEADS, s, QK_PAD), _BF),
            jax.ShapeDtypeStruct((b, HEADS, s, VDIM), _BF),
            jax.ShapeDtypeStruct((b, s, SG_WIDTH), _BF),
        ],
        compiler_params=_cparams(("arbitrary", "arbitrary")),
        name="inproj",
    )(x, pos3, mod3, g1, win_ext, gq, wuq_ext, gkv, wukv, invf, gsg, wsg, bsg3, gsgo)


def _attn_kernel(q_ref, k_ref, v_ref, o_ref, s_scr, *, unroll):
    nk, tq, tk = s_scr.shape
    q = q_ref[0, 0]
    ones = jnp.ones((tk, LANES), _BF)

    def score_chunk(j, mrun):
        sc = _dot_nt(q, k_ref[0, 0, pl.ds(pl.multiple_of(j * tk, tk), tk), :])
        s_scr[j] = sc
        for c in range(tk // LANES):
            mrun = jnp.maximum(mrun, sc[:, c * LANES:(c + 1) * LANES])
        return mrun

    mrun = lax.fori_loop(0, nk, score_chunk, jnp.full((tq, LANES), -jnp.inf, _F32), unroll=unroll)
    m = jnp.max(mrun, axis=-1, keepdims=True)

    def value_chunk(j, acc):
        p = jnp.exp2(s_scr[j] - m).astype(_BF)
        vc = v_ref[0, 0, pl.ds(pl.multiple_of(j * tk, tk), tk), :]
        return acc + _dot(p, jnp.concatenate([vc, ones], axis=-1))

    acc = lax.fori_loop(0, nk, value_chunk, jnp.zeros((tq, VDIM + LANES), _F32), unroll=unroll)
    o_ref[0] = (acc[:, :VDIM] / acc[:, VDIM:]).astype(_BF)


def _attention(q, k, v, tq, tk):
    b, h, s, _ = q.shape
    return pl.pallas_call(
        functools.partial(_attn_kernel, unroll=8),
        grid=(b, h, s // tq),
        scratch_shapes=[pltpu.VMEM((s // tk, tq, tk), _F32)],
        in_specs=[
            pl.BlockSpec((1, 1, tq, QK_PAD), lambda bi, hi, i: (bi, hi, i, 0)),
            pl.BlockSpec((1, 1, s, QK_PAD), lambda bi, hi, i: (bi, hi, 0, 0)),
            pl.BlockSpec((1, 1, s, VDIM), lambda bi, hi, i: (bi, hi, 0, 0)),
        ],
        out_specs=pl.BlockSpec((1, tq, VDIM), lambda bi, hi, i: (bi, i, hi)),
        out_shape=jax.ShapeDtypeStruct((b, s, h * VDIM), _BF),
        compiler_params=_cparams(("arbitrary", "arbitrary", "arbitrary")),
        name="attention",
    )(q, k, v)


def _post_kernel(o_ref, ysg_ref, x_ref, mod_ref, gao_ref, wo_ref, g2_ref, wq_ref, keys_ref,
                 x1_ref, h2_ref, sc_ref):
    gate1 = mod_ref[0, 2:3, :]
    shift2 = mod_ref[0, 3:4, :]
    scale2 = mod_ref[0, 4:5, :]
    width = o_ref.shape[1]
    yn = _rms(o_ref[...].astype(_F32), gao_ref[...]).astype(_BF)
    y = _dot(yn, wo_ref[0:width, :]) + _dot(ysg_ref[...], wo_ref[width:, :])
    x1 = x_ref[...] + gate1 * y
    x1_ref[...] = x1
    h2 = (_rms(x1, g2_ref[...]) * (1.0 + scale2) + shift2).astype(_BF)
    h2_ref[...] = h2
    qp = _dot(h2, wq_ref[...]).astype(_BF)
    pad = jnp.zeros((SC_PITCH - N_KEYS, LANES), _F32)
    for hh in range(2 * PEER_HEADS):
        sc = _dot_nt(keys_ref[hh], qp[:, hh * HALF_DIM:(hh + 1) * HALF_DIM])
        for c in range(sc.shape[1] // LANES):
            sc_ref[hh, c * SC_PITCH:c * SC_PITCH + N_KEYS, :] = sc[:, c * LANES:(c + 1) * LANES]
            sc_ref[hh, c * SC_PITCH + N_KEYS:(c + 1) * SC_PITCH, :] = pad


def _post(o, ysg, x2d, mod3, gao, wo, g2, wq, keys, tt, s):
    t, d = x2d.shape
    width = o.shape[1]
    per_b = s // tt
    full = lambda *shape: pl.BlockSpec(shape, lambda i: (0,) * len(shape))
    return pl.pallas_call(
        _post_kernel,
        grid=(t // tt,),
        in_specs=[
            pl.BlockSpec((tt, width), lambda i: (i, 0)),
            pl.BlockSpec((tt, width), lambda i: (i, 0)),
            pl.BlockSpec((tt, d), lambda i: (i, 0)),
            pl.BlockSpec((1, 6, d), lambda i: (i // per_b, 0, 0)),
            full(1, width),
            full(2 * width, d),
            full(1, d),
            full(d, wq.shape[1]),
            full(2 * PEER_HEADS, N_KEYS, HALF_DIM),
        ],
        out_specs=[
            pl.BlockSpec((tt, d), lambda i: (i, 0)),
            pl.BlockSpec((tt, d), lambda i: (i, 0)),
            pl.BlockSpec((2 * PEER_HEADS, tt // LANES * SC_PITCH, LANES), lambda i: (0, i, 0)),
        ],
        out_shape=[
            jax.ShapeDtypeStruct((t, d), _F32),
            jax.ShapeDtypeStruct((t, d), _BF),
            jax.ShapeDtypeStruct((2 * PEER_HEADS, t // LANES * SC_PITCH, LANES), _F32),
        ],
        compiler_params=_cparams(("arbitrary",)),
        name="post",
    )(o, ysg, x2d, mod3, gao, wo, g2, wq, keys)


_NB = [TOPK // (a + 1) for a in range(TOPK)]


def _oddeven_merge(lo, hi, r):
    step = r * 2
    if step < hi - lo:
        yield from _oddeven_merge(lo, hi, step)
        yield from _oddeven_merge(lo + r, hi, step)
        yield from [(i, i + r) for i in range(lo + r, hi - r, step)]
    else:
        yield (lo, lo + r)


def _oddeven_merge_sort(lo, hi):
    if hi - lo >= 1:
        mid = lo + (hi - lo) // 2
        yield from _oddeven_merge_sort(lo, mid)
        yield from _oddeven_merge_sort(mid + 1, hi)
        yield from _oddeven_merge(lo, hi, 1)


_NET16 = tuple(_oddeven_merge_sort(0, TOPK - 1))

ROUTE_TILES = SUBLANES
ROUTE_TOKENS = ROUTE_TILES * LANES


def _beats(vb, ib, va, ia):
    if isinstance(ia, int) and isinstance(ib, int):
        return vb > va if ib > ia else vb >= va
    return (vb > va) | ((vb == va) & (ib < ia))


def _compare_exchange(v, i, a, b):
    swap = _beats(v[b], i[b], v[a], i[a])
    v[a], v[b] = jnp.where(swap, v[b], v[a]), jnp.where(swap, v[a], v[b])
    i[a], i[b] = jnp.where(swap, i[b], i[a]), jnp.where(swap, i[a], i[b])


def _merge_top16(va, ia, vb, ib):
    cv, ci = [], []
    for r in range(TOPK):
        o = TOPK - 1 - r
        take_b = _beats(vb[o], ib[o], va[r], ia[r])
        cv.append(jnp.where(take_b, vb[o], va[r]))
        ci.append(jnp.where(take_b, ib[o], ia[r]))
    dist = TOPK // 2
    while dist:
        for r in range(TOPK):
            if not r & dist:
                _compare_exchange(cv, ci, r, r + dist)
        dist //= 2
    return cv, ci


def _top16(load_key):
    groups = []
    for g in range(N_KEYS // TOPK):
        v = [load_key(g * TOPK + r) for r in range(TOPK)]
        i = [g * TOPK + r for r in range(TOPK)]
        for a, b in _NET16:
            _compare_exchange(v, i, a, b)
        groups.append((v, i))
    while len(groups) > 1:
        groups = [_merge_top16(*groups[n], *groups[n + 1]) for n in range(0, len(groups), 2)]
    return groups[0]


def _pair_top16(v1, i1, v2, i2):
    cells = [(a, b) for a in range(TOPK) for b in range(_NB[a])]
    cand = [v1[a] + v2[b] for a, b in cells]
    picked = []
    for _ in range(TOPK):
        best = cand[0]
        flat = jnp.zeros(best.shape, jnp.int32)
        for (a, b), val in zip(cells[1:], cand[1:]):
            take = val > best
            best = jnp.where(take, val, best)
            flat = jnp.where(take, a * TOPK + b, flat)
        cand = [jnp.where(flat == a * TOPK + b, -jnp.inf, val) for (a, b), val in zip(cells, cand)]
        a_sel = jnp.right_shift(flat, 4)
        b_sel = jnp.bitwise_and(flat, TOPK - 1)
        row, col = i1[TOPK - 1], i2[TOPK - 1]
        for n in range(TOPK - 2, -1, -1):
            row = jnp.where(a_sel == n, i1[n], row)
            col = jnp.where(b_sel == n, i2[n], col)
        picked.append((best, row, col))
    return picked


def _route_kernel(sc_ref, rows_ref, cols_ref, colst_ref, gate_ref, row_scr, col_scr, gate_scr):
    def one_head(hd, carry):
        def top(hh):
            return _top16(lambda k: sc_ref[hh, pl.ds(k, ROUTE_TILES, stride=SC_PITCH), :])

        v1, i1 = top(2 * hd)
        v2, i2 = top(2 * hd + 1)
        picked = _pair_top16(v1, i1, v2, i2)
        e = [jnp.exp(sc - picked[0][0]) for sc, _, _ in picked]
        z = e[0]
        for ek in e[1:]:
            z = z + ek
        for n, (_, row, col) in enumerate(picked):
            base = pl.multiple_of((hd * TOPK + n) * ROUTE_TILES, ROUTE_TILES)
            row_scr[pl.ds(base, ROUTE_TILES), :] = jnp.broadcast_to(row, z.shape).astype(jnp.int32)
            col_scr[pl.ds(base, ROUTE_TILES), :] = jnp.broadcast_to(col, z.shape).astype(jnp.int32)
            gate_scr[pl.ds(base, ROUTE_TILES), :] = e[n] / z
        return carry

    lax.fori_loop(0, PEER_HEADS, one_head, 0)
    for c in range(ROUTE_TILES):
        tok = slice(c * LANES, (c + 1) * LANES)
        rows_ref[tok, :] = row_scr[pl.ds(c, N_SEL, stride=ROUTE_TILES), :].T
        col_tile = col_scr[pl.ds(c, N_SEL, stride=ROUTE_TILES), :]
        cols_ref[tok, :] = col_tile.T
        colst_ref[c] = col_tile
        gate_ref[tok, :] = gate_scr[pl.ds(c, N_SEL, stride=ROUTE_TILES), :].T


def _route(scores):
    hh, rows, _ = scores.shape
    n_steps = rows // (ROUTE_TILES * SC_PITCH)
    t = n_steps * ROUTE_TOKENS
    return pl.pallas_call(
        _route_kernel,
        grid=(n_steps,),
        in_specs=[pl.BlockSpec((hh, ROUTE_TILES * SC_PITCH, LANES), lambda i: (0, i, 0))],
        out_specs=[
            pl.BlockSpec((ROUTE_TOKENS, N_SEL), lambda i: (i, 0)),
            pl.BlockSpec((ROUTE_TOKENS, N_SEL), lambda i: (i, 0)),
            pl.BlockSpec((ROUTE_TILES, N_SEL, LANES), lambda i: (i, 0, 0)),
            pl.BlockSpec((ROUTE_TOKENS, N_SEL), lambda i: (i, 0)),
        ],
        out_shape=[
            jax.ShapeDtypeStruct((t, N_SEL), jnp.int32),
            jax.ShapeDtypeStruct((t, N_SEL), jnp.int32),
            jax.ShapeDtypeStruct((t // LANES, N_SEL, LANES), jnp.int32),
            jax.ShapeDtypeStruct((t, N_SEL), _F32),
        ],
        scratch_shapes=[
            pltpu.VMEM((N_SEL * ROUTE_TILES, LANES), jnp.int32),
            pltpu.VMEM((N_SEL * ROUTE_TILES, LANES), jnp.int32),
            pltpu.VMEM((N_SEL * ROUTE_TILES, LANES), _F32),
        ],
        compiler_params=_cparams(("arbitrary",)),
        name="route",
    )(scores)


def _peer_kernel(h2_ref, rows_ref, cols_ref, colst_ref, gate_ref, u_ref, v_ref, x1_ref, mod_ref, gf_ref,
                 o_ref, g_scr, acc_scr):
    tb = h2_ref.shape[0]
    eb = u_ref.shape[0]
    half = tb // 2
    j = pl.program_id(1)

    @pl.when(j == 0)
    def _build_gates():
        acc_scr[...] = jnp.zeros_like(acc_scr)
        key_iota = lax.broadcasted_iota(jnp.int32, (N_KEYS, N_SEL), 0)
        lane_iota = lax.broadcasted_iota(jnp.int32, (N_SEL, N_KEYS), 1)

        def gated_rows(t):
            r = rows_ref[pl.ds(t, 1), :]
            g = gate_ref[pl.ds(t, 1), :]
            return jnp.where(key_iota == r, g, 0.0).astype(_BF)

        def gate_matrix_nt(t):
            cidx = cols_ref[pl.ds(t, 1), :]
            bmat = jnp.where(key_iota == cidx, 1.0, 0.0).astype(_BF)
            return _dot_nt(gated_rows(t), bmat)

        def gate_matrix_nn(t, col_of_slot):
            bmat = jnp.where(lane_iota == col_of_slot, 1.0, 0.0).astype(_BF)
            return _dot(gated_rows(t), bmat)

        def tile_pair(c, carry):
            ct = colst_ref[c]
            for t in range(LANES):
                pp = c * LANES + t
                base = pl.multiple_of(pp * G_PITCH, SUBLANES)
                g_scr[pl.ds(base, N_KEYS), :] = pltpu.pack_elementwise(
                    [gate_matrix_nn(pp, ct[:, t:t + 1]), gate_matrix_nt(pp + half)], packed_dtype=_BF)
            return carry

        lax.fori_loop(0, half // LANES, tile_pair, 0)

    a = _dot_nt(h2_ref[...], u_ref[...])
    gparts = []
    for ri in range(eb // N_KEYS):
        words = g_scr[pl.ds(j * (eb // N_KEYS) + ri, half, stride=G_PITCH), :]
        gparts.append(jnp.concatenate(
            [pltpu.unpack_elementwise(words, index=k, packed_dtype=_BF, unpacked_dtype=_F32)
             for k in range(2)], axis=0))
    gt = jnp.concatenate(gparts, axis=-1)
    w = (_gelu(a) * gt).astype(_BF)
    acc_scr[...] += _dot(w, v_ref[...])

    @pl.when(j == pl.num_programs(1) - 1)
    def _finish():
        gate2 = mod_ref[0, 5:6, :]
        x2 = x1_ref[...] + gate2 * acc_scr[...]
        o_ref[...] = _rms(x2, gf_ref[...])


def _peer(h2, rows, cols, colst, gates, u_bf, v_bf, x1, mod3, gf, tb, eb, s):
    t, d = h2.shape
    e = u_bf.shape[0]
    per_b = s // tb
    return pl.pallas_call(
        _peer_kernel,
        grid=(t // tb, e // eb),
        in_specs=[
            pl.BlockSpec((tb, d), lambda i, j: (i, 0)),
            pl.BlockSpec((tb, N_SEL), lambda i, j: (i, 0)),
            pl.BlockSpec((tb, N_SEL), lambda i, j: (i, 0)),
            pl.BlockSpec((tb // LANES, N_SEL, LANES), lambda i, j: (i, 0, 0)),
            pl.BlockSpec((tb, N_SEL), lambda i, j: (i, 0)),
            pl.BlockSpec((eb, d), lambda i, j: (j, 0)),
            pl.BlockSpec((eb, d), lambda i, j: (j, 0)),
            pl.BlockSpec((tb, d), lambda i, j: (i, 0)),
            pl.BlockSpec((1, 6, d), lambda i, j: (i // per_b, 0, 0)),
            pl.BlockSpec((1, d), lambda i, j: (0, 0)),
        ],
        out_specs=pl.BlockSpec((tb, d), lambda i, j: (i, 0)),
        out_shape=jax.ShapeDtypeStruct((t, d), _F32),
        scratch_shapes=[
            pltpu.VMEM((tb // 2 * G_PITCH, N_KEYS), jnp.uint32),
            pltpu.VMEM((tb, d), _F32),
        ],
        compiler_params=_cparams(("arbitrary", "arbitrary")),
        name="peer",
    )(h2, rows, cols, colst, gates, u_bf, v_bf, x1, mod3, gf)


def _pad_cols(w, width):
    return jnp.pad(w, ((0, 0), (0, width - w.shape[1])))


def _rot_cols(w):
    half = w.shape[1] // 2
    return jnp.concatenate([-w[:, half:], w[:, :half]], axis=1)


def kernel(x, c, positions, w_ada, b_ada, g_norm1, w_in, g_q_a, w_uq, g_kv_a, w_ukv, g_sg, w_sg, b_sg,
           g_attn_out, g_sg_out, w_o, g_norm2, w_peer_q, peer_keys, peer_u, peer_v, g_final):
    b, s, d = x.shape
    depth = w_ada.shape[0]
    assert depth == 1, "the final norm is fused into the last layer's expert kernel"
    n_mod = w_ada.shape[2] // d
    t = b * s

    inv_freq = 1.0 / (ROPE_THETA ** (jnp.arange(0, ROPE, 2, dtype=_F32) / ROPE))
    invf = _pad_cols(jnp.concatenate([inv_freq, inv_freq])[None, :], LANES)
    pos3 = positions[:, :, None]
    c_pad = jnp.pad(c, ((0, SUBLANES - b), (0, 0)))

    o1 = Q_RANK
    o2 = o1 + KV_RANK
    o3 = o2 + ROPE
    xs = x
    for l in range(depth):
        mod = _adaln(c_pad, w_ada[l], b_ada[l][None, :])[:b]
        mod3 = mod.reshape(b, n_mod, d)

        w_kr = w_in[l][:, o2:o3]
        win_ext = jnp.concatenate(
            [w_in[l][:, :o2], _pad_cols(w_kr, LANES), _pad_cols(_rot_cols(w_kr), LANES), w_in[l][:, o3:]],
            axis=1).astype(_BF)
        wq3 = w_uq[l].reshape(Q_RANK, HEADS, QK_DIM)
        wq_nope = wq3[:, :, :NOPE].reshape(Q_RANK, HEADS * NOPE)
        wq_rope = [wq3[:, hd, NOPE:] for hd in range(HEADS)]
        wuq_ext = jnp.concatenate(
            [wq_nope] + [_pad_cols(w, LANES) for w in wq_rope]
            + [_pad_cols(_rot_cols(w), LANES) for w in wq_rope], axis=1).astype(_BF)

        q, k, v, ysg = _inproj(
            xs, pos3, mod3, g_norm1[l][None, :], win_ext, g_q_a[l][None, :], wuq_ext,
            g_kv_a[l][None, :], w_ukv[l].astype(_BF), invf, g_sg[l][None, :], w_sg[l].astype(_BF),
            b_sg[l][:, :, None], g_sg_out[l][None, :], ts=512)
        o = _attention(q, k, v, tq=1024, tk=256)

        keys = peer_keys[l].reshape(2 * PEER_HEADS, N_KEYS, HALF_DIM).astype(_BF)
        x1, h2, scores = _post(
            o.reshape(t, HEADS * VDIM), ysg.reshape(t, SG_WIDTH), xs.reshape(t, d), mod3,
            g_attn_out[l][None, :], w_o[l].astype(_BF), g_norm2[l][None, :], w_peer_q[l].astype(_BF),
            keys, tt=256, s=s)
        rows, cols, colst, gates = _route(scores)
        out = _peer(h2, rows, cols, colst, gates, peer_u[l].astype(_BF), peer_v[l].astype(_BF), x1, mod3,
                    g_final[None, :], tb=512, eb=2048, s=s)
        xs = out.reshape(b, s, d)
    return xs
```

```python
import functools
import math

import jax
import jax.numpy as jnp
from jax import lax
from jax.experimental import pallas as pl
from jax.experimental.pallas import tpu as pltpu

LANES = 128
SUBLANES = 8
VMEM_LIMIT_BYTES = 56 * 1024 * 1024

EPS = 1e-6
ROPE_THETA = 10000.0
HEADS = 4
NOPE = 128
ROPE = 64
VDIM = 128
QK_DIM = NOPE + ROPE
QK_PAD = 2 * LANES
Q_RANK = 256
KV_RANK = 128
SG_HEADS = 4
SG_DIM = 128
SG_CHUNK = 128
SG_WIDTH = SG_HEADS * SG_DIM
PEER_HEADS = 8
N_KEYS = 128
TOPK = 16
HALF_DIM = 128
N_SEL = PEER_HEADS * TOPK

G_PITCH = N_KEYS + SUBLANES
SC_PITCH = N_KEYS + SUBLANES

_BF = jnp.bfloat16
_F32 = jnp.float32


def _cparams(sem):
    return pltpu.CompilerParams(dimension_semantics=sem, vmem_limit_bytes=VMEM_LIMIT_BYTES)


def _rms(x, g):
    return x * lax.rsqrt(jnp.mean(x * x, axis=-1, keepdims=True) + EPS) * g


def _gelu(x):
    return 0.5 * x * (1.0 + lax.erf(x * (1.0 / math.sqrt(2.0))))


def _dot(a, b):
    return jnp.dot(a, b, preferred_element_type=_F32)


def _dot_nt(a, b):
    return lax.dot_general(a, b, (((1,), (1,)), ((), ())), preferred_element_type=_F32)


def _adaln_kernel(c_ref, w_ref, b_ref, o_ref):
    c = c_ref[...]
    c_act = (c * jax.nn.sigmoid(c)).astype(_BF)
    o_ref[...] = _dot(c_act, w_ref[...].astype(_BF)) + b_ref[...]


def _adaln(c_pad, w_ada, b_ada):
    rows, d = c_pad.shape
    n = w_ada.shape[1]
    tn = 1024
    return pl.pallas_call(
        _adaln_kernel,
        grid=(n // tn,),
        in_specs=[
            pl.BlockSpec((rows, d), lambda j: (0, 0)),
            pl.BlockSpec((d, tn), lambda j: (0, j)),
            pl.BlockSpec((1, tn), lambda j: (0, j)),
        ],
        out_specs=pl.BlockSpec((rows, tn), lambda j: (0, j)),
        out_shape=jax.ShapeDtypeStruct((rows, n), _F32),
        compiler_params=_cparams(("arbitrary",)),
        name="adaln",
    )(c_pad, w_ada, b_ada)


def _inproj_kernel(x_ref, pos_ref, mod_ref, g1_ref, win_ref, gq_ref, wuq_ref, gkv_ref, wukv_ref,
                   invf_ref, gsg_ref, wsg_ref, bsg_ref, gsgo_ref,
                   q_ref, k_ref, v_ref, ysg_ref, *, scale):
    ts = x_ref.shape[1]
    x = x_ref[0]
    shift1 = mod_ref[0, 0:1, :]
    scale1 = mod_ref[0, 1:2, :]
    h = _rms(x, g1_ref[...]) * (1.0 + scale1) + shift1
    z = _dot(h.astype(_BF), win_ref[...])

    o_kv = Q_RANK
    o_kr = o_kv + KV_RANK
    o_krot = o_kr + LANES
    o_u = o_krot + LANES
    o_v = o_u + SG_WIDTH

    ang = pos_ref[0].astype(_F32) * invf_ref[...]
    cos2 = jnp.cos(ang)
    sin2 = jnp.sin(ang)

    cq = _rms(z[:, :Q_RANK], gq_ref[...]).astype(_BF)
    qall = _dot(cq, wuq_ref[...])
    for hd in range(HEADS):
        q_nope = qall[:, hd * NOPE:(hd + 1) * NOPE]
        q_r = qall[:, HEADS * NOPE + hd * LANES: HEADS * NOPE + (hd + 1) * LANES]
        q_rot = qall[:, HEADS * (NOPE + LANES) + hd * LANES: HEADS * (NOPE + LANES) + (hd + 1) * LANES]
        q_ref[0, hd, :, 0:LANES] = (q_nope * scale).astype(_BF)
        q_ref[0, hd, :, LANES:QK_PAD] = ((q_r * cos2 + q_rot * sin2) * scale).astype(_BF)

    ckv = _rms(z[:, o_kv:o_kr], gkv_ref[...]).astype(_BF)
    kv = _dot(ckv, wukv_ref[...])
    k_rope = (z[:, o_kr:o_krot] * cos2 + z[:, o_krot:o_u] * sin2).astype(_BF)
    for hd in range(HEADS):
        base = hd * (NOPE + VDIM)
        k_ref[0, hd, :, 0:LANES] = kv[:, base:base + NOPE].astype(_BF)
        k_ref[0, hd, :, LANES:QK_PAD] = k_rope
        v_ref[0, hd] = kv[:, base + NOPE:base + NOPE + VDIM].astype(_BF)

    u = _gelu(z[:, o_u:o_v])
    vg = _rms(_gelu(z[:, o_v:o_v + SG_WIDTH]), gsg_ref[...]).astype(_BF)
    for ch in range(ts // SG_CHUNK):
        r0 = ch * SG_CHUNK
        parts = []
        for hd in range(SG_HEADS):
            c0 = hd * SG_DIM
            mix = _dot(wsg_ref[hd], vg[r0:r0 + SG_CHUNK, c0:c0 + SG_DIM]) + bsg_ref[hd]
            parts.append(u[r0:r0 + SG_CHUNK, c0:c0 + SG_DIM] * mix)
        y = jnp.concatenate(parts, axis=-1)
        ysg_ref[0, r0:r0 + SG_CHUNK, :] = _rms(y, gsgo_ref[...]).astype(_BF)


def _inproj(x, pos3, mod3, g1, win_ext, gq, wuq_ext, gkv, wukv, invf, gsg, wsg, bsg3, gsgo, ts):
    b, s, d = x.shape
    nz = win_ext.shape[1]
    full = lambda *shape: pl.BlockSpec(shape, lambda bi, i: (0,) * len(shape))
    return pl.pallas_call(
        functools.partial(_inproj_kernel, scale=QK_DIM ** -0.5 * math.log2(math.e)),
        grid=(b, s // ts),
        in_specs=[
            pl.BlockSpec((1, ts, d), lambda bi, i: (bi, i, 0)),
            pl.BlockSpec((1, ts, 1), lambda bi, i: (bi, i, 0)),
            pl.BlockSpec((1, 6, d), lambda bi, i: (bi, 0, 0)),
            full(1, d),
            full(d, nz),
            full(1, Q_RANK),
            full(Q_RANK, wuq_ext.shape[1]),
            full(1, KV_RANK),
            full(KV_RANK, wukv.shape[1]),
            full(1, LANES),
            full(1, SG_WIDTH),
            full(SG_HEADS, SG_CHUNK, SG_CHUNK),
            full(SG_HEADS, SG_CHUNK, 1),
            full(1, SG_WIDTH),
        ],
        out_specs=[
            pl.BlockSpec((1, HEADS, ts, QK_PAD), lambda bi, i: (bi, 0, i, 0)),
            pl.BlockSpec((1, HEADS, ts, QK_PAD), lambda bi, i: (bi, 0, i, 0)),
            pl.BlockSpec((1, HEADS, ts, VDIM), lambda bi, i: (bi, 0, i, 0)),
            pl.BlockSpec((1, ts, SG_WIDTH), lambda bi, i: (bi, i, 0)),
        ],
        out_shape=[
            jax.ShapeDtypeStruct((b, HEADS, s, QK_PAD), _BF),
            jax.ShapeDtypeStruct((b, HEADS, s, QK_PAD), _BF),
            jax.ShapeDtypeStruct((b, HEADS, s, VDIM), _BF),
            jax.ShapeDtypeStruct((b, s, SG_WIDTH), _BF),
        ],
        compiler_params=_cparams(("arbitrary", "arbitrary")),
        name="inproj",
    )(x, pos3, mod3, g1, win_ext, gq, wuq_ext, gkv, wukv, invf, gsg, wsg, bsg3, gsgo)


def _attn_kernel(q_ref, k_ref, v_ref, o_ref, s_scr, *, unroll):
    nk, tq, tk = s_scr.shape
    q = q_ref[0, 0]
    ones = jnp.ones((tk, LANES), _BF)

    def score_chunk(j, mrun):
        sc = _dot_nt(q, k_ref[0, 0, pl.ds(pl.multiple_of(j * tk, tk), tk), :])
        s_scr[j] = sc
        for c in range(tk // LANES):
            mrun = jnp.maximum(mrun, sc[:, c * LANES:(c + 1) * LANES])
        return mrun

    mrun = lax.fori_loop(0, nk, score_chunk, jnp.full((tq, LANES), -jnp.inf, _F32), unroll=unroll)
    m = jnp.max(mrun, axis=-1, keepdims=True)

    def value_chunk(j, acc):
        p = jnp.exp2(s_scr[j] - m).astype(_BF)
        vc = v_ref[0, 0, pl.ds(pl.multiple_of(j * tk, tk), tk), :]
        return acc + _dot(p, jnp.concatenate([vc, ones], axis=-1))

    acc = lax.fori_loop(0, nk, value_chunk, jnp.zeros((tq, VDIM + LANES), _F32), unroll=unroll)
    o_ref[0] = (acc[:, :VDIM] / acc[:, VDIM:]).astype(_BF)


def _attention(q, k, v, tq, tk):
    b, h, s, _ = q.shape
    return pl.pallas_call(
        functools.partial(_attn_kernel, unroll=8),
        grid=(b, h, s // tq),
        scratch_shapes=[pltpu.VMEM((s // tk, tq, tk), _F32)],
        in_specs=[
            pl.BlockSpec((1, 1, tq, QK_PAD), lambda bi, hi, i: (bi, hi, i, 0)),
            pl.BlockSpec((1, 1, s, QK_PAD), lambda bi, hi, i: (bi, hi, 0, 0)),
            pl.BlockSpec((1, 1, s, VDIM), lambda bi, hi, i: (bi, hi, 0, 0)),
        ],
        out_specs=pl.BlockSpec((1, tq, VDIM), lambda bi, hi, i: (bi, i, hi)),
        out_shape=jax.ShapeDtypeStruct((b, s, h * VDIM), _BF),
        compiler_params=_cparams(("arbitrary", "arbitrary", "arbitrary")),
        name="attention",
    )(q, k, v)


def _post_kernel(o_ref, ysg_ref, x_ref, mod_ref, gao_ref, wo_ref, g2_ref, wq_ref, keys_ref,
                 x1_ref, h2_ref, sc_ref):
    gate1 = mod_ref[0, 2:3, :]
    shift2 = mod_ref[0, 3:4, :]
    scale2 = mod_ref[0, 4:5, :]
    width = o_ref.shape[1]
    yn = _rms(o_ref[...].astype(_F32), gao_ref[...]).astype(_BF)
    y = _dot(yn, wo_ref[0:width, :]) + _dot(ysg_ref[...], wo_ref[width:, :])
    x1 = x_ref[...] + gate1 * y
    x1_ref[...] = x1
    h2 = (_rms(x1, g2_ref[...]) * (1.0 + scale2) + shift2).astype(_BF)
    h2_ref[...] = h2
    qp = _dot(h2, wq_ref[...]).astype(_BF)
    pad = jnp.zeros((SC_PITCH - N_KEYS, LANES), _F32)
    tiles = qp.shape[0] // LANES
    for hh in range(2 * PEER_HEADS):
        sc = _dot_nt(keys_ref[hh], qp[:, hh * HALF_DIM:(hh + 1) * HALF_DIM])
        for c in range(tiles):
            r0 = ((hh % 2) * tiles + c) * SC_PITCH
            sc_ref[0, hh // 2, r0:r0 + N_KEYS, :] = sc[:, c * LANES:(c + 1) * LANES]
            sc_ref[0, hh // 2, r0 + N_KEYS:r0 + SC_PITCH, :] = pad


def _post(o, ysg, x2d, mod3, gao, wo, g2, wq, keys, tt, s):
    t, d = x2d.shape
    width = o.shape[1]
    per_b = s // tt
    full = lambda *shape: pl.BlockSpec(shape, lambda i: (0,) * len(shape))
    return pl.pallas_call(
        _post_kernel,
        grid=(t // tt,),
        in_specs=[
            pl.BlockSpec((tt, width), lambda i: (i, 0)),
            pl.BlockSpec((tt, width), lambda i: (i, 0)),
            pl.BlockSpec((tt, d), lambda i: (i, 0)),
            pl.BlockSpec((1, 6, d), lambda i: (i // per_b, 0, 0)),
            full(1, width),
            full(2 * width, d),
            full(1, d),
            full(d, wq.shape[1]),
            full(2 * PEER_HEADS, N_KEYS, HALF_DIM),
        ],
        out_specs=[
            pl.BlockSpec((tt, d), lambda i: (i, 0)),
            pl.BlockSpec((tt, d), lambda i: (i, 0)),
            pl.BlockSpec((1, PEER_HEADS, 2 * tt // LANES * SC_PITCH, LANES), lambda i: (i, 0, 0, 0)),
        ],
        out_shape=[
            jax.ShapeDtypeStruct((t, d), _F32),
            jax.ShapeDtypeStruct((t, d), _BF),
            jax.ShapeDtypeStruct((t // tt, PEER_HEADS, 2 * tt // LANES * SC_PITCH, LANES), _F32),
        ],
        compiler_params=_cparams(("arbitrary",)),
        name="post",
    )(o, ysg, x2d, mod3, gao, wo, g2, wq, keys)


_NB = [TOPK // (a + 1) for a in range(TOPK)]


def _oddeven_merge(lo, hi, r):
    step = r * 2
    if step < hi - lo:
        yield from _oddeven_merge(lo, hi, step)
        yield from _oddeven_merge(lo + r, hi, step)
        yield from [(i, i + r) for i in range(lo + r, hi - r, step)]
    else:
        yield (lo, lo + r)


def _oddeven_merge_sort(lo, hi):
    if hi - lo >= 1:
        mid = lo + (hi - lo) // 2
        yield from _oddeven_merge_sort(lo, mid)
        yield from _oddeven_merge_sort(mid + 1, hi)
        yield from _oddeven_merge(lo, hi, 1)


_NET16 = tuple(_oddeven_merge_sort(0, TOPK - 1))

ROUTE_TILES = SUBLANES // 2
ROUTE_TOKENS = ROUTE_TILES * LANES


def _beats(vb, ib, va, ia):
    if isinstance(ia, int) and isinstance(ib, int):
        return vb > va if ib > ia else vb >= va
    return (vb > va) | ((vb == va) & (ib < ia))


def _compare_exchange(v, i, a, b):
    swap = _beats(v[b], i[b], v[a], i[a])
    v[a], v[b] = jnp.where(swap, v[b], v[a]), jnp.where(swap, v[a], v[b])
    i[a], i[b] = jnp.where(swap, i[b], i[a]), jnp.where(swap, i[a], i[b])


def _merge_top16(va, ia, vb, ib):
    cv, ci = [], []
    for r in range(TOPK):
        o = TOPK - 1 - r
        take_b = _beats(vb[o], ib[o], va[r], ia[r])
        cv.append(jnp.where(take_b, vb[o], va[r]))
        ci.append(jnp.where(take_b, ib[o], ia[r]))
    dist = TOPK // 2
    while dist:
        for r in range(TOPK):
            if not r & dist:
                _compare_exchange(cv, ci, r, r + dist)
        dist //= 2
    return cv, ci


def _top16(load_key):
    groups = []
    for g in range(N_KEYS // TOPK):
        v = [load_key(g * TOPK + r) for r in range(TOPK)]
        i = [g * TOPK + r for r in range(TOPK)]
        for a, b in _NET16:
            _compare_exchange(v, i, a, b)
        groups.append((v, i))
    while len(groups) > 1:
        groups = [_merge_top16(*groups[n], *groups[n + 1]) for n in range(0, len(groups), 2)]
    return groups[0]


def _pair_top16(v1, i1, v2, i2):
    cells = [(a, b) for a in range(TOPK) for b in range(_NB[a])]
    cand = [v1[a] + v2[b] for a, b in cells]
    picked = []
    for _ in range(TOPK):
        best = cand[0]
        flat = jnp.zeros(best.shape, jnp.int32)
        for (a, b), val in zip(cells[1:], cand[1:]):
            take = val > best
            best = jnp.where(take, val, best)
            flat = jnp.where(take, a * TOPK + b, flat)
        cand = [jnp.where(flat == a * TOPK + b, -jnp.inf, val) for (a, b), val in zip(cells, cand)]
        a_sel = jnp.right_shift(flat, 4)
        b_sel = jnp.bitwise_and(flat, TOPK - 1)
        row, col = i1[TOPK - 1], i2[TOPK - 1]
        for n in range(TOPK - 2, -1, -1):
            row = jnp.where(a_sel == n, i1[n], row)
            col = jnp.where(b_sel == n, i2[n], col)
        picked.append((best, row, col))
    return picked


def _route_head(sc_ref, hd, buf, row_scr, col_scr, gate_scr):
    v, i = _top16(lambda k: sc_ref[0, 0, pl.ds(k, SUBLANES, stride=SC_PITCH), :])
    v2 = [pltpu.roll(x, ROUTE_TILES, axis=0) for x in v]
    i2 = [pltpu.roll(x, ROUTE_TILES, axis=0) for x in i]
    picked = _pair_top16(v, i, v2, i2)
    e = [jnp.exp(sc - picked[0][0]) for sc, _, _ in picked]
    z = e[0]
    for ek in e[1:]:
        z = z + ek
    for n, (_, row, col) in enumerate(picked):
        base = pl.multiple_of((hd * TOPK + n) * SUBLANES, SUBLANES)
        row_scr[buf, pl.ds(base, SUBLANES), :] = row
        col_scr[buf, pl.ds(base, SUBLANES), :] = col
        gate_scr[buf, pl.ds(base, SUBLANES), :] = e[n] / z


def _peer_kernel(h2_ref, sc_ref, u_ref, v_ref, x1_ref, mod_ref, gf_ref, o_ref,
                 g_scr, acc_scr, row_scr, col_scr, gate_scr, rows_t, cols_t, gates_t):
    tb = h2_ref.shape[0]
    eb = u_ref.shape[0]
    half = tb // 2
    step = pl.program_id(0)
    j = pl.program_id(1)

    @pl.when(step == 0)
    def _route_first_block():
        _route_head(sc_ref, j, 0, row_scr, col_scr, gate_scr)

    @pl.when(step > 0)
    def _evaluate_and_route():
        buf = (step - 1) % 2

        @pl.when(j == 0)
        def _build_gates():
            acc_scr[...] = jnp.zeros_like(acc_scr)
            for c in range(ROUTE_TILES):
                tok = slice(c * LANES, (c + 1) * LANES)
                rows_t[tok, :] = row_scr[buf, pl.ds(c, N_SEL, stride=SUBLANES), :].T
                cols_t[tok, :] = col_scr[buf, pl.ds(c, N_SEL, stride=SUBLANES), :].T
                gates_t[tok, :] = gate_scr[buf, pl.ds(c, N_SEL, stride=SUBLANES), :].T
            key_iota = lax.broadcasted_iota(jnp.int32, (N_KEYS, N_SEL), 0)
            lane_iota = lax.broadcasted_iota(jnp.int32, (N_SEL, N_KEYS), 1)

            def gated_rows(t):
                r = rows_t[pl.ds(t, 1), :]
                g = gates_t[pl.ds(t, 1), :]
                return jnp.where(key_iota == r, g, 0.0).astype(_BF)

            def gate_matrix_nt(t):
                cidx = cols_t[pl.ds(t, 1), :]
                bmat = jnp.where(key_iota == cidx, 1.0, 0.0).astype(_BF)
                return _dot_nt(gated_rows(t), bmat)

            def gate_matrix_nn(t, col_of_slot):
                bmat = jnp.where(lane_iota == col_of_slot, 1.0, 0.0).astype(_BF)
                return _dot(gated_rows(t), bmat)

            def tile_pair(c, carry):
                ct = col_scr[buf, pl.ds(c, N_SEL, stride=SUBLANES), :]
                for t in range(LANES):
                    pp = c * LANES + t
                    base = pl.multiple_of(pp * G_PITCH, SUBLANES)
                    g_scr[pl.ds(base, N_KEYS), :] = pltpu.pack_elementwise(
                        [gate_matrix_nn(pp, ct[:, t:t + 1]), gate_matrix_nt(pp + half)],
                        packed_dtype=_BF)
                return carry

            lax.fori_loop(0, half // LANES, tile_pair, 0)

        _route_head(sc_ref, j, step % 2, row_scr, col_scr, gate_scr)
        a = _dot_nt(h2_ref[...], u_ref[...])
        gparts = []
        for ri in range(eb // N_KEYS):
            words = g_scr[pl.ds(j * (eb // N_KEYS) + ri, half, stride=G_PITCH), :]
            gparts.append(jnp.concatenate(
                [pltpu.unpack_elementwise(words, index=k, packed_dtype=_BF, unpacked_dtype=_F32)
                 for k in range(2)], axis=0))
        gt = jnp.concatenate(gparts, axis=-1)
        w = (_gelu(a) * gt).astype(_BF)
        acc_scr[...] += _dot(w, v_ref[...])

        @pl.when(j == pl.num_programs(1) - 1)
        def _finish():
            gate2 = mod_ref[0, 5:6, :]
            x2 = x1_ref[...] + gate2 * acc_scr[...]
            o_ref[...] = _rms(x2, gf_ref[...])


def _peer(h2, scores, u_bf, v_bf, x1, mod3, gf, eb, s):
    t, d = h2.shape
    e = u_bf.shape[0]
    tb = ROUTE_TOKENS
    nb = t // tb
    per_b = s // tb
    assert e // eb == PEER_HEADS, "one routing head is scheduled per expert block"
    evaluated = lambda i: jnp.maximum(i - 1, 0)
    routed = lambda i: jnp.minimum(i, nb - 1)
    return pl.pallas_call(
        _peer_kernel,
        grid=(nb + 1, PEER_HEADS),
        in_specs=[
            pl.BlockSpec((tb, d), lambda i, j: (evaluated(i), 0)),
            pl.BlockSpec((1, 1) + scores.shape[2:], lambda i, j: (routed(i), j, 0, 0)),
            pl.BlockSpec((eb, d), lambda i, j: (j, 0)),
            pl.BlockSpec((eb, d), lambda i, j: (j, 0)),
            pl.BlockSpec((tb, d), lambda i, j: (evaluated(i), 0)),
            pl.BlockSpec((1, 6, d), lambda i, j: (evaluated(i) // per_b, 0, 0)),
            pl.BlockSpec((1, d), lambda i, j: (0, 0)),
        ],
        out_specs=pl.BlockSpec((tb, d), lambda i, j: (evaluated(i), 0)),
        out_shape=jax.ShapeDtypeStruct((t, d), _F32),
        scratch_shapes=[
            pltpu.VMEM((tb // 2 * G_PITCH, N_KEYS), jnp.uint32),
            pltpu.VMEM((tb, d), _F32),
            pltpu.VMEM((2, N_SEL * SUBLANES, LANES), jnp.int32),
            pltpu.VMEM((2, N_SEL * SUBLANES, LANES), jnp.int32),
            pltpu.VMEM((2, N_SEL * SUBLANES, LANES), _F32),
            pltpu.VMEM((tb, N_SEL), jnp.int32),
            pltpu.VMEM((tb, N_SEL), jnp.int32),
            pltpu.VMEM((tb, N_SEL), _F32),
        ],
        compiler_params=_cparams(("arbitrary", "arbitrary")),
        name="peer",
    )(h2, scores, u_bf, v_bf, x1, mod3, gf)


def _pad_cols(w, width):
    return jnp.pad(w, ((0, 0), (0, width - w.shape[1])))


def _rot_cols(w):
    half = w.shape[1] // 2
    return jnp.concatenate([-w[:, half:], w[:, :half]], axis=1)


def kernel(x, c, positions, w_ada, b_ada, g_norm1, w_in, g_q_a, w_uq, g_kv_a, w_ukv, g_sg, w_sg, b_sg,
           g_attn_out, g_sg_out, w_o, g_norm2, w_peer_q, peer_keys, peer_u, peer_v, g_final):
    b, s, d = x.shape
    depth = w_ada.shape[0]
    assert depth == 1, "the final norm is fused into the last layer's expert kernel"
    n_mod = w_ada.shape[2] // d
    t = b * s

    inv_freq = 1.0 / (ROPE_THETA ** (jnp.arange(0, ROPE, 2, dtype=_F32) / ROPE))
    invf = _pad_cols(jnp.concatenate([inv_freq, inv_freq])[None, :], LANES)
    pos3 = positions[:, :, None]
    c_pad = jnp.pad(c, ((0, SUBLANES - b), (0, 0)))

    o1 = Q_RANK
    o2 = o1 + KV_RANK
    o3 = o2 + ROPE
    xs = x
    for l in range(depth):
        mod = _adaln(c_pad, w_ada[l], b_ada[l][None, :])[:b]
        mod3 = mod.reshape(b, n_mod, d)

        w_kr = w_in[l][:, o2:o3]
        win_ext = jnp.concatenate(
            [w_in[l][:, :o2], _pad_cols(w_kr, LANES), _pad_cols(_rot_cols(w_kr), LANES), w_in[l][:, o3:]],
            axis=1).astype(_BF)
        wq3 = w_uq[l].reshape(Q_RANK, HEADS, QK_DIM)
        wq_nope = wq3[:, :, :NOPE].reshape(Q_RANK, HEADS * NOPE)
        wq_rope = [wq3[:, hd, NOPE:] for hd in range(HEADS)]
        wuq_ext = jnp.concatenate(
            [wq_nope] + [_pad_cols(w, LANES) for w in wq_rope]
            + [_pad_cols(_rot_cols(w), LANES) for w in wq_rope], axis=1).astype(_BF)

        q, k, v, ysg = _inproj(
            xs, pos3, mod3, g_norm1[l][None, :], win_ext, g_q_a[l][None, :], wuq_ext,
            g_kv_a[l][None, :], w_ukv[l].astype(_BF), invf, g_sg[l][None, :], w_sg[l].astype(_BF),
            b_sg[l][:, :, None], g_sg_out[l][None, :], ts=512)
        o = _attention(q, k, v, tq=1024, tk=256)

        keys = peer_keys[l].reshape(2 * PEER_HEADS, N_KEYS, HALF_DIM).astype(_BF)
        x1, h2, scores = _post(
            o.reshape(t, HEADS * VDIM), ysg.reshape(t, SG_WIDTH), xs.reshape(t, d), mod3,
            g_attn_out[l][None, :], w_o[l].astype(_BF), g_norm2[l][None, :], w_peer_q[l].astype(_BF),
            keys, tt=ROUTE_TOKENS, s=s)
        out = _peer(h2, scores, peer_u[l].astype(_BF), peer_v[l].astype(_BF), x1, mod3,
                    g_final[None, :], eb=peer_u.shape[1] // PEER_HEADS, s=s)
        xs = out.reshape(b, s, d)
    return xs
```

```python
import functools
import math

import jax
import jax.numpy as jnp
from jax import lax
from jax.experimental import pallas as pl
from jax.experimental.pallas import tpu as pltpu

LANES = 128
SUBLANES = 8
VMEM_LIMIT_BYTES = 60 * 1024 * 1024

EPS = 1e-6
ROPE_THETA = 10000.0
HEADS = 4
NOPE = 128
ROPE = 64
VDIM = 128
QK_DIM = NOPE + ROPE
QK_PAD = 2 * LANES
Q_RANK = 256
KV_RANK = 128
SG_HEADS = 4
SG_DIM = 128
SG_CHUNK = 128
SG_WIDTH = SG_HEADS * SG_DIM
PEER_HEADS = 8
N_KEYS = 128
TOPK = 16
HALF_DIM = 128
N_SEL = PEER_HEADS * TOPK

G_PITCH = N_KEYS + SUBLANES
SC_PITCH = N_KEYS + SUBLANES

_BF = jnp.bfloat16
_F32 = jnp.float32


def _cparams(sem):
    return pltpu.CompilerParams(dimension_semantics=sem, vmem_limit_bytes=VMEM_LIMIT_BYTES)


def _rms(x, g):
    return x * lax.rsqrt(jnp.mean(x * x, axis=-1, keepdims=True) + EPS) * g


def _gelu(x):
    return 0.5 * x * (1.0 + lax.erf(x * (1.0 / math.sqrt(2.0))))


def _dot(a, b):
    return jnp.dot(a, b, preferred_element_type=_F32)


def _dot_nt(a, b):
    return lax.dot_general(a, b, (((1,), (1,)), ((), ())), preferred_element_type=_F32)


def _adaln_kernel(c_ref, w_ref, b_ref, o_ref):
    c = c_ref[...]
    c_act = (c * jax.nn.sigmoid(c)).astype(_BF)
    o_ref[...] = _dot(c_act, w_ref[...].astype(_BF)) + b_ref[...]


def _adaln(c_pad, w_ada, b_ada):
    rows, d = c_pad.shape
    n = w_ada.shape[1]
    tn = 1024
    return pl.pallas_call(
        _adaln_kernel,
        grid=(n // tn,),
        in_specs=[
            pl.BlockSpec((rows, d), lambda j: (0, 0)),
            pl.BlockSpec((d, tn), lambda j: (0, j)),
            pl.BlockSpec((1, tn), lambda j: (0, j)),
        ],
        out_specs=pl.BlockSpec((rows, tn), lambda j: (0, j)),
        out_shape=jax.ShapeDtypeStruct((rows, n), _F32),
        compiler_params=_cparams(("arbitrary",)),
        name="adaln",
    )(c_pad, w_ada, b_ada)


def _inproj_kernel(x_ref, pos_ref, mod_ref, g1_ref, win_ref, gq_ref, wuq_ref, gkv_ref, wukv_ref,
                   invf_ref, gsg_ref, wsg_ref, bsg_ref, gsgo_ref,
                   q_ref, k_ref, v_ref, ysg_ref, *, scale):
    ts = x_ref.shape[1]
    x = x_ref[0]
    shift1 = mod_ref[0, 0:1, :]
    scale1 = mod_ref[0, 1:2, :]
    h = _rms(x, g1_ref[...]) * (1.0 + scale1) + shift1
    z = _dot(h.astype(_BF), win_ref[...])

    o_kv = Q_RANK
    o_kr = o_kv + KV_RANK
    o_krot = o_kr + LANES
    o_u = o_krot + LANES
    o_v = o_u + SG_WIDTH

    ang = pos_ref[0].astype(_F32) * invf_ref[...]
    cos2 = jnp.cos(ang)
    sin2 = jnp.sin(ang)

    cq = _rms(z[:, :Q_RANK], gq_ref[...]).astype(_BF)
    qall = _dot(cq, wuq_ref[...])
    for hd in range(HEADS):
        q_nope = qall[:, hd * NOPE:(hd + 1) * NOPE]
        q_r = qall[:, HEADS * NOPE + hd * LANES: HEADS * NOPE + (hd + 1) * LANES]
        q_rot = qall[:, HEADS * (NOPE + LANES) + hd * LANES: HEADS * (NOPE + LANES) + (hd + 1) * LANES]
        q_ref[0, hd, :, 0:LANES] = (q_nope * scale).astype(_BF)
        q_ref[0, hd, :, LANES:QK_PAD] = ((q_r * cos2 + q_rot * sin2) * scale).astype(_BF)

    ckv = _rms(z[:, o_kv:o_kr], gkv_ref[...]).astype(_BF)
    kv = _dot(ckv, wukv_ref[...])
    k_rope = (z[:, o_kr:o_krot] * cos2 + z[:, o_krot:o_u] * sin2).astype(_BF)
    for hd in range(HEADS):
        base = hd * (NOPE + VDIM)
        k_ref[0, hd, :, 0:LANES] = kv[:, base:base + NOPE].astype(_BF)
        k_ref[0, hd, :, LANES:QK_PAD] = k_rope
        v_ref[0, hd] = kv[:, base + NOPE:base + NOPE + VDIM].astype(_BF)

    u = _gelu(z[:, o_u:o_v])
    vg = _rms(_gelu(z[:, o_v:o_v + SG_WIDTH]), gsg_ref[...]).astype(_BF)
    for ch in range(ts // SG_CHUNK):
        r0 = ch * SG_CHUNK
        parts = []
        for hd in range(SG_HEADS):
            c0 = hd * SG_DIM
            mix = _dot(wsg_ref[hd], vg[r0:r0 + SG_CHUNK, c0:c0 + SG_DIM]) + bsg_ref[hd]
            parts.append(u[r0:r0 + SG_CHUNK, c0:c0 + SG_DIM] * mix)
        y = jnp.concatenate(parts, axis=-1)
        ysg_ref[0, r0:r0 + SG_CHUNK, :] = _rms(y, gsgo_ref[...]).astype(_BF)


def _inproj(x, pos3, mod3, g1, win_ext, gq, wuq_ext, gkv, wukv, invf, gsg, wsg, bsg3, gsgo, ts):
    b, s, d = x.shape
    nz = win_ext.shape[1]
    full = lambda *shape: pl.BlockSpec(shape, lambda bi, i: (0,) * len(shape))
    return pl.pallas_call(
        functools.partial(_inproj_kernel, scale=QK_DIM ** -0.5 * math.log2(math.e)),
        grid=(b, s // ts),
        in_specs=[
            pl.BlockSpec((1, ts, d), lambda bi, i: (bi, i, 0)),
            pl.BlockSpec((1, ts, 1), lambda bi, i: (bi, i, 0)),
            pl.BlockSpec((1, 6, d), lambda bi, i: (bi, 0, 0)),
            full(1, d),
            full(d, nz),
            full(1, Q_RANK),
            full(Q_RANK, wuq_ext.shape[1]),
            full(1, KV_RANK),
            full(KV_RANK, wukv.shape[1]),
            full(1, LANES),
            full(1, SG_WIDTH),
            full(SG_HEADS, SG_CHUNK, SG_CHUNK),
            full(SG_HEADS, SG_CHUNK, 1),
            full(1, SG_WIDTH),
        ],
        out_specs=[
            pl.BlockSpec((1, HEADS, ts, QK_PAD), lambda bi, i: (bi, 0, i, 0)),
            pl.BlockSpec((1, HEADS, ts, QK_PAD), lambda bi, i: (bi, 0, i, 0)),
            pl.BlockSpec((1, HEADS, ts, VDIM), lambda bi, i: (bi, 0, i, 0)),
            pl.BlockSpec((1, ts, SG_WIDTH), lambda bi, i: (bi, i, 0)),
        ],
        out_shape=[
            jax.ShapeDtypeStruct((b, HEADS, s, QK_PAD), _BF),
            jax.ShapeDtypeStruct((b, HEADS, s, QK_PAD), _BF),
            jax.ShapeDtypeStruct((b, HEADS, s, VDIM), _BF),
            jax.ShapeDtypeStruct((b, s, SG_WIDTH), _BF),
        ],
        compiler_params=_cparams(("arbitrary", "arbitrary")),
        name="inproj",
    )(x, pos3, mod3, g1, win_ext, gq, wuq_ext, gkv, wukv, invf, gsg, wsg, bsg3, gsgo)


def _attn_kernel(q_ref, k_ref, v_ref, uf_ref, vf_ref, o_ref, ubf_ref, vbf_ref, s_scr, *, unroll):
    nk, tq, tk = s_scr.shape
    q = q_ref[0, 0]
    ones = jnp.ones((tk, LANES), _BF)

    def score_chunk(j, mrun):
        sc = _dot_nt(q, k_ref[0, 0, pl.ds(pl.multiple_of(j * tk, tk), tk), :])
        s_scr[j] = sc
        for c in range(tk // LANES):
            mrun = jnp.maximum(mrun, sc[:, c * LANES:(c + 1) * LANES])
        return mrun

    mrun = lax.fori_loop(0, nk, score_chunk, jnp.full((tq, LANES), -jnp.inf, _F32), unroll=unroll)
    m = jnp.max(mrun, axis=-1, keepdims=True)

    def value_chunk(j, acc):
        p = jnp.exp2(s_scr[j] - m).astype(_BF)
        vc = v_ref[0, 0, pl.ds(pl.multiple_of(j * tk, tk), tk), :]
        return acc + _dot(p, jnp.concatenate([vc, ones], axis=-1))

    acc = lax.fori_loop(0, nk, value_chunk, jnp.zeros((tq, VDIM + LANES), _F32), unroll=unroll)
    o_ref[0] = (acc[:, :VDIM] / acc[:, VDIM:]).astype(_BF)
    ubf_ref[...] = uf_ref[...].astype(_BF)
    vbf_ref[...] = vf_ref[...].astype(_BF)


def _attention(q, k, v, u_f32, v_f32, tq, tk):
    b, h, s, _ = q.shape
    nq = s // tq
    e, d = u_f32.shape
    er = e // (b * h * nq)
    step = lambda bi, hi, i: ((bi * h + hi) * nq + i, 0)
    return pl.pallas_call(
        functools.partial(_attn_kernel, unroll=8),
        grid=(b, h, nq),
        scratch_shapes=[pltpu.VMEM((s // tk, tq, tk), _F32)],
        in_specs=[
            pl.BlockSpec((1, 1, tq, QK_PAD), lambda bi, hi, i: (bi, hi, i, 0)),
            pl.BlockSpec((1, 1, s, QK_PAD), lambda bi, hi, i: (bi, hi, 0, 0)),
            pl.BlockSpec((1, 1, s, VDIM), lambda bi, hi, i: (bi, hi, 0, 0)),
            pl.BlockSpec((er, d), step),
            pl.BlockSpec((er, d), step),
        ],
        out_specs=[
            pl.BlockSpec((1, tq, VDIM), lambda bi, hi, i: (bi, i, hi)),
            pl.BlockSpec((er, d), step),
            pl.BlockSpec((er, d), step),
        ],
        out_shape=[
            jax.ShapeDtypeStruct((b, s, h * VDIM), _BF),
            jax.ShapeDtypeStruct((e, d), _BF),
            jax.ShapeDtypeStruct((e, d), _BF),
        ],
        compiler_params=_cparams(("arbitrary", "arbitrary", "arbitrary")),
        name="attention",
    )(q, k, v, u_f32, v_f32)


def _post_kernel(o_ref, ysg_ref, x_ref, mod_ref, gao_ref, wo_ref, g2_ref, wq_ref, keys_ref,
                 x1_ref, h2_ref, sc_ref):
    gate1 = mod_ref[0, 2:3, :]
    shift2 = mod_ref[0, 3:4, :]
    scale2 = mod_ref[0, 4:5, :]
    width = o_ref.shape[1]
    yn = _rms(o_ref[...].astype(_F32), gao_ref[...]).astype(_BF)
    y = _dot(yn, wo_ref[0:width, :]) + _dot(ysg_ref[...], wo_ref[width:, :])
    x1 = x_ref[...] + gate1 * y
    x1_ref[...] = x1
    h2 = (_rms(x1, g2_ref[...]) * (1.0 + scale2) + shift2).astype(_BF)
    h2_ref[...] = h2
    qp = _dot(h2, wq_ref[...]).astype(_BF)
    pad = jnp.zeros((SC_PITCH - N_KEYS, LANES), _F32)
    tiles = qp.shape[0] // LANES
    for hh in range(2 * PEER_HEADS):
        sc = _dot_nt(keys_ref[hh], qp[:, hh * HALF_DIM:(hh + 1) * HALF_DIM])
        for c in range(tiles):
            r0 = ((hh % 2) * tiles + c) * SC_PITCH
            sc_ref[0, hh // 2, r0:r0 + N_KEYS, :] = sc[:, c * LANES:(c + 1) * LANES]
            sc_ref[0, hh // 2, r0 + N_KEYS:r0 + SC_PITCH, :] = pad


def _post(o, ysg, x2d, mod3, gao, wo, g2, wq, keys, tt, s):
    t, d = x2d.shape
    width = o.shape[1]
    per_b = s // tt
    full = lambda *shape: pl.BlockSpec(shape, lambda i: (0,) * len(shape))
    return pl.pallas_call(
        _post_kernel,
        grid=(t // tt,),
        in_specs=[
            pl.BlockSpec((tt, width), lambda i: (i, 0)),
            pl.BlockSpec((tt, width), lambda i: (i, 0)),
            pl.BlockSpec((tt, d), lambda i: (i, 0)),
            pl.BlockSpec((1, 6, d), lambda i: (i // per_b, 0, 0)),
            full(1, width),
            full(2 * width, d),
            full(1, d),
            full(d, wq.shape[1]),
            full(2 * PEER_HEADS, N_KEYS, HALF_DIM),
        ],
        out_specs=[
            pl.BlockSpec((tt, d), lambda i: (i, 0)),
            pl.BlockSpec((tt, d), lambda i: (i, 0)),
            pl.BlockSpec((1, PEER_HEADS, 2 * tt // LANES * SC_PITCH, LANES), lambda i: (i, 0, 0, 0)),
        ],
        out_shape=[
            jax.ShapeDtypeStruct((t, d), _F32),
            jax.ShapeDtypeStruct((t, d), _BF),
            jax.ShapeDtypeStruct((t // tt, PEER_HEADS, 2 * tt // LANES * SC_PITCH, LANES), _F32),
        ],
        compiler_params=_cparams(("arbitrary",)),
        name="post",
    )(o, ysg, x2d, mod3, gao, wo, g2, wq, keys)


_NB = [TOPK // (a + 1) for a in range(TOPK)]


def _oddeven_merge(lo, hi, r):
    step = r * 2
    if step < hi - lo:
        yield from _oddeven_merge(lo, hi, step)
        yield from _oddeven_merge(lo + r, hi, step)
        yield from [(i, i + r) for i in range(lo + r, hi - r, step)]
    else:
        yield (lo, lo + r)


def _oddeven_merge_sort(lo, hi):
    if hi - lo >= 1:
        mid = lo + (hi - lo) // 2
        yield from _oddeven_merge_sort(lo, mid)
        yield from _oddeven_merge_sort(mid + 1, hi)
        yield from _oddeven_merge(lo, hi, 1)


_NET16 = tuple(_oddeven_merge_sort(0, TOPK - 1))

ROUTE_TILES = SUBLANES // 2
ROUTE_TOKENS = ROUTE_TILES * LANES


def _beats(vb, ib, va, ia):
    if isinstance(ia, int) and isinstance(ib, int):
        return vb > va if ib > ia else vb >= va
    return (vb > va) | ((vb == va) & (ib < ia))


def _compare_exchange(v, i, a, b):
    swap = _beats(v[b], i[b], v[a], i[a])
    v[a], v[b] = jnp.where(swap, v[b], v[a]), jnp.where(swap, v[a], v[b])
    i[a], i[b] = jnp.where(swap, i[b], i[a]), jnp.where(swap, i[a], i[b])


def _merge_top16(va, ia, vb, ib):
    cv, ci = [], []
    for r in range(TOPK):
        o = TOPK - 1 - r
        take_b = _beats(vb[o], ib[o], va[r], ia[r])
        cv.append(jnp.where(take_b, vb[o], va[r]))
        ci.append(jnp.where(take_b, ib[o], ia[r]))
    dist = TOPK // 2
    while dist:
        for r in range(TOPK):
            if not r & dist:
                _compare_exchange(cv, ci, r, r + dist)
        dist //= 2
    return cv, ci


def _sorted_groups(load_key, which):
    groups = []
    for g in which:
        v = [load_key(g * TOPK + r) for r in range(TOPK)]
        i = [g * TOPK + r for r in range(TOPK)]
        for a, b in _NET16:
            _compare_exchange(v, i, a, b)
        groups.append((v, i))
    return groups


def _top16(groups):
    while len(groups) > 1:
        groups = [_merge_top16(*groups[n], *groups[n + 1]) for n in range(0, len(groups), 2)]
    return groups[0]


def _pair_top16(v1, i1, v2, i2):
    cells = [(a, b) for a in range(TOPK) for b in range(_NB[a])]
    cand = [v1[a] + v2[b] for a, b in cells]
    picked = []
    for _ in range(TOPK):
        best = cand[0]
        flat = jnp.zeros(best.shape, jnp.int32)
        for (a, b), val in zip(cells[1:], cand[1:]):
            take = val > best
            best = jnp.where(take, val, best)
            flat = jnp.where(take, a * TOPK + b, flat)
        cand = [jnp.where(flat == a * TOPK + b, -jnp.inf, val) for (a, b), val in zip(cells, cand)]
        a_sel = jnp.right_shift(flat, 4)
        b_sel = jnp.bitwise_and(flat, TOPK - 1)
        row, col = i1[TOPK - 1], i2[TOPK - 1]
        for n in range(TOPK - 2, -1, -1):
            row = jnp.where(a_sel == n, i1[n], row)
            col = jnp.where(b_sel == n, i2[n], col)
        picked.append((best, row, col))
    return picked


def _route_sort(sc_ref, which):
    return _sorted_groups(lambda k: sc_ref[0, 0, pl.ds(k, SUBLANES, stride=SC_PITCH), :], which)


def _route_select(v, i, hd, buf, row_scr, col_scr, gate_scr):
    v2 = [pltpu.roll(x, ROUTE_TILES, axis=0) for x in v]
    i2 = [pltpu.roll(x, ROUTE_TILES, axis=0) for x in i]
    picked = _pair_top16(v, i, v2, i2)
    e = [jnp.exp(sc - picked[0][0]) for sc, _, _ in picked]
    z = e[0]
    for ek in e[1:]:
        z = z + ek
    for n, (_, row, col) in enumerate(picked):
        base = pl.multiple_of((hd * TOPK + n) * SUBLANES, SUBLANES)
        row_scr[buf, pl.ds(base, SUBLANES), :] = row
        col_scr[buf, pl.ds(base, SUBLANES), :] = col
        gate_scr[buf, pl.ds(base, SUBLANES), :] = e[n] / z


def _peer_kernel(h2_ref, sc_ref, u_ref, v_ref, x1_ref, mod_ref, gf_ref, o_ref,
                 g_scr, acc_scr, row_scr, col_scr, gate_scr, rows_t, cols_t, gates_t):
    tb = h2_ref.shape[0]
    eb = u_ref.shape[0]
    half = tb // 2
    step = pl.program_id(0)
    j = pl.program_id(1)

    @pl.when(step == 0)
    def _route_first_block():
        v, i = _top16(_route_sort(sc_ref, range(N_KEYS // TOPK)))
        _route_select(v, i, j, 0, row_scr, col_scr, gate_scr)

    @pl.when(step > 0)
    def _evaluate_and_route():
        buf = (step - 1) % 2

        @pl.when(j == 0)
        def _build_gates():
            acc_scr[...] = jnp.zeros_like(acc_scr)
            for c in range(ROUTE_TILES):
                tok = slice(c * LANES, (c + 1) * LANES)
                rows_t[tok, :] = row_scr[buf, pl.ds(c, N_SEL, stride=SUBLANES), :].T
                cols_t[tok, :] = col_scr[buf, pl.ds(c, N_SEL, stride=SUBLANES), :].T
                gates_t[tok, :] = gate_scr[buf, pl.ds(c, N_SEL, stride=SUBLANES), :].T
            key_iota = lax.broadcasted_iota(jnp.int32, (N_KEYS, N_SEL), 0)
            lane_iota = lax.broadcasted_iota(jnp.int32, (N_SEL, N_KEYS), 1)

            def gated_rows(t):
                r = rows_t[pl.ds(t, 1), :]
                g = gates_t[pl.ds(t, 1), :]
                return jnp.where(key_iota == r, g, 0.0).astype(_BF)

            def gate_matrix_nt(t):
                cidx = cols_t[pl.ds(t, 1), :]
                bmat = jnp.where(key_iota == cidx, 1.0, 0.0).astype(_BF)
                return _dot_nt(gated_rows(t), bmat)

            def gate_matrix_nn(t, col_of_slot):
                bmat = jnp.where(lane_iota == col_of_slot, 1.0, 0.0).astype(_BF)
                return _dot(gated_rows(t), bmat)

            def tile_pair(c, carry):
                ct = col_scr[buf, pl.ds(c, N_SEL, stride=SUBLANES), :]
                for t in range(LANES):
                    pp = c * LANES + t
                    base = pl.multiple_of(pp * G_PITCH, SUBLANES)
                    g_scr[pl.ds(base, N_KEYS), :] = pltpu.pack_elementwise(
                        [gate_matrix_nn(pp, ct[:, t:t + 1]), gate_matrix_nt(pp + half)],
                        packed_dtype=_BF)
                return carry

            lax.fori_loop(0, half // LANES, tile_pair, 0)

        groups = _route_sort(sc_ref, range(N_KEYS // TOPK))
        a = _dot_nt(h2_ref[...], u_ref[...])
        gparts = []
        for ri in range(eb // N_KEYS):
            words = g_scr[pl.ds(j * (eb // N_KEYS) + ri, half, stride=G_PITCH), :]
            gparts.append(jnp.concatenate(
                [pltpu.unpack_elementwise(words, index=k, packed_dtype=_BF, unpacked_dtype=_F32)
                 for k in range(2)], axis=0))
        w = (_gelu(a) * jnp.concatenate(gparts, axis=-1)).astype(_BF)
        top_v, top_i = _top16(groups)
        _route_select(top_v, top_i, j, step % 2, row_scr, col_scr, gate_scr)
        acc_scr[...] += _dot(w, v_ref[...])

        @pl.when(j == pl.num_programs(1) - 1)
        def _finish():
            gate2 = mod_ref[0, 5:6, :]
            x2 = x1_ref[...] + gate2 * acc_scr[...]
            o_ref[...] = _rms(x2, gf_ref[...])


def _peer(h2, scores, u_bf, v_bf, x1, mod3, gf, eb, s):
    t, d = h2.shape
    e = u_bf.shape[0]
    tb = ROUTE_TOKENS
    nb = t // tb
    per_b = s // tb
    assert e // eb == PEER_HEADS, "one routing head is scheduled per expert block"
    evaluated = lambda i: jnp.maximum(i - 1, 0)
    routed = lambda i: jnp.minimum(i, nb - 1)
    return pl.pallas_call(
        _peer_kernel,
        grid=(nb + 1, PEER_HEADS),
        in_specs=[
            pl.BlockSpec((tb, d), lambda i, j: (evaluated(i), 0)),
            pl.BlockSpec((1, 1) + scores.shape[2:], lambda i, j: (routed(i), j, 0, 0)),
            pl.BlockSpec((eb, d), lambda i, j: (j, 0)),
            pl.BlockSpec((eb, d), lambda i, j: (j, 0)),
            pl.BlockSpec((tb, d), lambda i, j: (evaluated(i), 0)),
            pl.BlockSpec((1, 6, d), lambda i, j: (evaluated(i) // per_b, 0, 0)),
            pl.BlockSpec((1, d), lambda i, j: (0, 0)),
        ],
        out_specs=pl.BlockSpec((tb, d), lambda i, j: (evaluated(i), 0)),
        out_shape=jax.ShapeDtypeStruct((t, d), _F32),
        scratch_shapes=[
            pltpu.VMEM((tb // 2 * G_PITCH, N_KEYS), jnp.uint32),
            pltpu.VMEM((tb, d), _F32),
            pltpu.VMEM((2, N_SEL * SUBLANES, LANES), jnp.int32),
            pltpu.VMEM((2, N_SEL * SUBLANES, LANES), jnp.int32),
            pltpu.VMEM((2, N_SEL * SUBLANES, LANES), _F32),
            pltpu.VMEM((tb, N_SEL), jnp.int32),
            pltpu.VMEM((tb, N_SEL), jnp.int32),
            pltpu.VMEM((tb, N_SEL), _F32),
        ],
        compiler_params=_cparams(("arbitrary", "arbitrary")),
        name="peer",
    )(h2, scores, u_bf, v_bf, x1, mod3, gf)


def _pad_cols(w, width):
    return jnp.pad(w, ((0, 0), (0, width - w.shape[1])))


def _rot_cols(w):
    half = w.shape[1] // 2
    return jnp.concatenate([-w[:, half:], w[:, :half]], axis=1)


def kernel(x, c, positions, w_ada, b_ada, g_norm1, w_in, g_q_a, w_uq, g_kv_a, w_ukv, g_sg, w_sg, b_sg,
           g_attn_out, g_sg_out, w_o, g_norm2, w_peer_q, peer_keys, peer_u, peer_v, g_final):
    b, s, d = x.shape
    depth = w_ada.shape[0]
    assert depth == 1, "the final norm is fused into the last layer's expert kernel"
    n_mod = w_ada.shape[2] // d
    t = b * s

    inv_freq = 1.0 / (ROPE_THETA ** (jnp.arange(0, ROPE, 2, dtype=_F32) / ROPE))
    invf = _pad_cols(jnp.concatenate([inv_freq, inv_freq])[None, :], LANES)
    pos3 = positions[:, :, None]
    c_pad = jnp.pad(c, ((0, SUBLANES - b), (0, 0)))

    o1 = Q_RANK
    o2 = o1 + KV_RANK
    o3 = o2 + ROPE
    xs = x
    for l in range(depth):
        mod = _adaln(c_pad, w_ada[l], b_ada[l][None, :])[:b]
        mod3 = mod.reshape(b, n_mod, d)

        w_kr = w_in[l][:, o2:o3]
        win_ext = jnp.concatenate(
            [w_in[l][:, :o2], _pad_cols(w_kr, LANES), _pad_cols(_rot_cols(w_kr), LANES), w_in[l][:, o3:]],
            axis=1).astype(_BF)
        wq3 = w_uq[l].reshape(Q_RANK, HEADS, QK_DIM)
        wq_nope = wq3[:, :, :NOPE].reshape(Q_RANK, HEADS * NOPE)
        wq_rope = [wq3[:, hd, NOPE:] for hd in range(HEADS)]
        wuq_ext = jnp.concatenate(
            [wq_nope] + [_pad_cols(w, LANES) for w in wq_rope]
            + [_pad_cols(_rot_cols(w), LANES) for w in wq_rope], axis=1).astype(_BF)

        q, k, v, ysg = _inproj(
            xs, pos3, mod3, g_norm1[l][None, :], win_ext, g_q_a[l][None, :], wuq_ext,
            g_kv_a[l][None, :], w_ukv[l].astype(_BF), invf, g_sg[l][None, :], w_sg[l].astype(_BF),
            b_sg[l][:, :, None], g_sg_out[l][None, :], ts=512)
        o, u_bf, v_bf = _attention(q, k, v, peer_u[l], peer_v[l], tq=1024, tk=256)

        keys = peer_keys[l].reshape(2 * PEER_HEADS, N_KEYS, HALF_DIM).astype(_BF)
        x1, h2, scores = _post(
            o.reshape(t, HEADS * VDIM), ysg.reshape(t, SG_WIDTH), xs.reshape(t, d), mod3,
            g_attn_out[l][None, :], w_o[l].astype(_BF), g_norm2[l][None, :], w_peer_q[l].astype(_BF),
            keys, tt=ROUTE_TOKENS, s=s)
        out = _peer(h2, scores, u_bf, v_bf, x1, mod3,
                    g_final[None, :], eb=peer_u.shape[1] // PEER_HEADS, s=s)
        xs = out.reshape(b, s, d)
    return xs
```

```python
import functools
import math

import jax
import jax.numpy as jnp
from jax import lax
from jax.experimental import pallas as pl
from jax.experimental.pallas import tpu as pltpu

LANES = 128
SUBLANES = 8
VMEM_LIMIT_BYTES = 60 * 1024 * 1024

EPS = 1e-6
ROPE_THETA = 10000.0
HEADS = 4
NOPE = 128
ROPE = 64
VDIM = 128
QK_DIM = NOPE + ROPE
QK_PAD = 2 * LANES
Q_RANK = 256
KV_RANK = 128
SG_HEADS = 4
SG_DIM = 128
SG_CHUNK = 128
SG_WIDTH = SG_HEADS * SG_DIM
PEER_HEADS = 8
N_KEYS = 128
TOPK = 16
HALF_DIM = 128
N_SEL = PEER_HEADS * TOPK

G_PITCH = N_KEYS + SUBLANES
SC_PITCH = N_KEYS + SUBLANES

ADALN_COLS = 1024
INPROJ_ROWS = 512
ATTN_Q_ROWS = 1024
ATTN_K_ROWS = 256
ATTN_UNROLL = 8

_BF = jnp.bfloat16
_F32 = jnp.float32


def _cparams(sem):
    return pltpu.CompilerParams(dimension_semantics=sem, vmem_limit_bytes=VMEM_LIMIT_BYTES)


def _rms(x, g):
    return x * lax.rsqrt(jnp.mean(x * x, axis=-1, keepdims=True) + EPS) * g


def _gelu(x):
    return 0.5 * x * (1.0 + lax.erf(x * (1.0 / math.sqrt(2.0))))


def _dot(a, b):
    return jnp.dot(a, b, preferred_element_type=_F32)


def _dot_nt(a, b):
    return lax.dot_general(a, b, (((1,), (1,)), ((), ())), preferred_element_type=_F32)


def _adaln_kernel(c_ref, w_ref, b_ref, o_ref):
    c = c_ref[...]
    c_act = (c * jax.nn.sigmoid(c)).astype(_BF)
    o_ref[...] = _dot(c_act, w_ref[...].astype(_BF)) + b_ref[...]


def _adaln(c_pad, w_ada, b_ada):
    rows, d = c_pad.shape
    n = w_ada.shape[1]
    tn = ADALN_COLS
    return pl.pallas_call(
        _adaln_kernel,
        grid=(n // tn,),
        in_specs=[
            pl.BlockSpec((rows, d), lambda j: (0, 0)),
            pl.BlockSpec((d, tn), lambda j: (0, j)),
            pl.BlockSpec((1, tn), lambda j: (0, j)),
        ],
        out_specs=pl.BlockSpec((rows, tn), lambda j: (0, j)),
        out_shape=jax.ShapeDtypeStruct((rows, n), _F32),
        compiler_params=_cparams(("arbitrary",)),
        name="adaln",
    )(c_pad, w_ada, b_ada)


def _inproj_kernel(x_ref, pos_ref, mod_ref, g1_ref, win_ref, gq_ref, wuq_ref, gkv_ref, wukv_ref,
                   invf_ref, gsg_ref, wsg_ref, bsg_ref, gsgo_ref,
                   q_ref, k_ref, v_ref, ysg_ref, *, scale):
    ts = x_ref.shape[1]
    x = x_ref[0]
    shift1 = mod_ref[0, 0:1, :]
    scale1 = mod_ref[0, 1:2, :]
    h = _rms(x, g1_ref[...]) * (1.0 + scale1) + shift1
    z = _dot(h.astype(_BF), win_ref[...])

    o_kv = Q_RANK
    o_kr = o_kv + KV_RANK
    o_krot = o_kr + LANES
    o_u = o_krot + LANES
    o_v = o_u + SG_WIDTH

    ang = pos_ref[0].astype(_F32) * invf_ref[...]
    cos2 = jnp.cos(ang)
    sin2 = jnp.sin(ang)

    cq = _rms(z[:, :Q_RANK], gq_ref[...]).astype(_BF)
    qall = _dot(cq, wuq_ref[...])
    for hd in range(HEADS):
        q_nope = qall[:, hd * NOPE:(hd + 1) * NOPE]
        q_r = qall[:, HEADS * NOPE + hd * LANES: HEADS * NOPE + (hd + 1) * LANES]
        q_rot = qall[:, HEADS * (NOPE + LANES) + hd * LANES: HEADS * (NOPE + LANES) + (hd + 1) * LANES]
        q_ref[0, hd, :, 0:LANES] = (q_nope * scale).astype(_BF)
        q_ref[0, hd, :, LANES:QK_PAD] = ((q_r * cos2 + q_rot * sin2) * scale).astype(_BF)

    ckv = _rms(z[:, o_kv:o_kr], gkv_ref[...]).astype(_BF)
    kv = _dot(ckv, wukv_ref[...])
    k_rope = (z[:, o_kr:o_krot] * cos2 + z[:, o_krot:o_u] * sin2).astype(_BF)
    for hd in range(HEADS):
        base = hd * (NOPE + VDIM)
        k_ref[0, hd, :, 0:LANES] = kv[:, base:base + NOPE].astype(_BF)
        k_ref[0, hd, :, LANES:QK_PAD] = k_rope
        v_ref[0, hd] = kv[:, base + NOPE:base + NOPE + VDIM].astype(_BF)

    u = _gelu(z[:, o_u:o_v])
    vg = _rms(_gelu(z[:, o_v:o_v + SG_WIDTH]), gsg_ref[...]).astype(_BF)
    for ch in range(ts // SG_CHUNK):
        r0 = ch * SG_CHUNK
        parts = []
        for hd in range(SG_HEADS):
            c0 = hd * SG_DIM
            mix = _dot(wsg_ref[hd], vg[r0:r0 + SG_CHUNK, c0:c0 + SG_DIM]) + bsg_ref[hd]
            parts.append(u[r0:r0 + SG_CHUNK, c0:c0 + SG_DIM] * mix)
        y = jnp.concatenate(parts, axis=-1)
        ysg_ref[0, r0:r0 + SG_CHUNK, :] = _rms(y, gsgo_ref[...]).astype(_BF)


def _inproj(x, pos3, mod3, g1, win_ext, gq, wuq_ext, gkv, wukv, invf, gsg, wsg, bsg3, gsgo, ts):
    b, s, d = x.shape
    nz = win_ext.shape[1]
    full = lambda *shape: pl.BlockSpec(shape, lambda bi, i: (0,) * len(shape))
    return pl.pallas_call(
        functools.partial(_inproj_kernel, scale=QK_DIM ** -0.5 * math.log2(math.e)),
        grid=(b, s // ts),
        in_specs=[
            pl.BlockSpec((1, ts, d), lambda bi, i: (bi, i, 0)),
            pl.BlockSpec((1, ts, 1), lambda bi, i: (bi, i, 0)),
            pl.BlockSpec((1, 6, d), lambda bi, i: (bi, 0, 0)),
            full(1, d),
            full(d, nz),
            full(1, Q_RANK),
            full(Q_RANK, wuq_ext.shape[1]),
            full(1, KV_RANK),
            full(KV_RANK, wukv.shape[1]),
            full(1, LANES),
            full(1, SG_WIDTH),
            full(SG_HEADS, SG_CHUNK, SG_CHUNK),
            full(SG_HEADS, SG_CHUNK, 1),
            full(1, SG_WIDTH),
        ],
        out_specs=[
            pl.BlockSpec((1, HEADS, ts, QK_PAD), lambda bi, i: (bi, 0, i, 0)),
            pl.BlockSpec((1, HEADS, ts, QK_PAD), lambda bi, i: (bi, 0, i, 0)),
            pl.BlockSpec((1, HEADS, ts, VDIM), lambda bi, i: (bi, 0, i, 0)),
            pl.BlockSpec((1, ts, SG_WIDTH), lambda bi, i: (bi, i, 0)),
        ],
        out_shape=[
            jax.ShapeDtypeStruct((b, HEADS, s, QK_PAD), _BF),
            jax.ShapeDtypeStruct((b, HEADS, s, QK_PAD), _BF),
            jax.ShapeDtypeStruct((b, HEADS, s, VDIM), _BF),
            jax.ShapeDtypeStruct((b, s, SG_WIDTH), _BF),
        ],
        compiler_params=_cparams(("arbitrary", "arbitrary")),
        name="inproj",
    )(x, pos3, mod3, g1, win_ext, gq, wuq_ext, gkv, wukv, invf, gsg, wsg, bsg3, gsgo)


def _attn_kernel(q_ref, k_ref, v_ref, uf_ref, vf_ref, o_ref, ubf_ref, vbf_ref, s_scr, *, unroll):
    nk, tq, tk = s_scr.shape
    q = q_ref[0, 0]
    ones = jnp.ones((tk, LANES), _BF)

    def score_chunk(j, mrun):
        sc = _dot_nt(q, k_ref[0, 0, pl.ds(pl.multiple_of(j * tk, tk), tk), :])
        s_scr[j] = sc
        for c in range(tk // LANES):
            mrun = jnp.maximum(mrun, sc[:, c * LANES:(c + 1) * LANES])
        return mrun

    mrun = lax.fori_loop(0, nk, score_chunk, jnp.full((tq, LANES), -jnp.inf, _F32), unroll=unroll)
    m = jnp.max(mrun, axis=-1, keepdims=True)

    def value_chunk(j, acc):
        p = jnp.exp2(s_scr[j] - m).astype(_BF)
        vc = v_ref[0, 0, pl.ds(pl.multiple_of(j * tk, tk), tk), :]
        return acc + _dot(p, jnp.concatenate([vc, ones], axis=-1))

    acc = lax.fori_loop(0, nk, value_chunk, jnp.zeros((tq, VDIM + LANES), _F32), unroll=unroll)
    o_ref[0] = (acc[:, :VDIM] / acc[:, VDIM:]).astype(_BF)
    ubf_ref[...] = uf_ref[...].astype(_BF)
    vbf_ref[...] = vf_ref[...].astype(_BF)


def _attention(q, k, v, u_f32, v_f32, tq, tk):
    b, h, s, _ = q.shape
    nq = s // tq
    e, d = u_f32.shape
    er = e // (b * h * nq)
    step = lambda bi, hi, i: ((bi * h + hi) * nq + i, 0)
    return pl.pallas_call(
        functools.partial(_attn_kernel, unroll=ATTN_UNROLL),
        grid=(b, h, nq),
        scratch_shapes=[pltpu.VMEM((s // tk, tq, tk), _F32)],
        in_specs=[
            pl.BlockSpec((1, 1, tq, QK_PAD), lambda bi, hi, i: (bi, hi, i, 0)),
            pl.BlockSpec((1, 1, s, QK_PAD), lambda bi, hi, i: (bi, hi, 0, 0)),
            pl.BlockSpec((1, 1, s, VDIM), lambda bi, hi, i: (bi, hi, 0, 0)),
            pl.BlockSpec((er, d), step),
            pl.BlockSpec((er, d), step),
        ],
        out_specs=[
            pl.BlockSpec((1, tq, VDIM), lambda bi, hi, i: (bi, i, hi)),
            pl.BlockSpec((er, d), step),
            pl.BlockSpec((er, d), step),
        ],
        out_shape=[
            jax.ShapeDtypeStruct((b, s, h * VDIM), _BF),
            jax.ShapeDtypeStruct((e, d), _BF),
            jax.ShapeDtypeStruct((e, d), _BF),
        ],
        compiler_params=_cparams(("arbitrary", "arbitrary", "arbitrary")),
        name="attention",
    )(q, k, v, u_f32, v_f32)


def _post_kernel(o_ref, ysg_ref, x_ref, mod_ref, gao_ref, wo_ref, g2_ref, wq_ref, keys_ref,
                 x1_ref, h2_ref, sc_ref):
    gate1 = mod_ref[0, 2:3, :]
    shift2 = mod_ref[0, 3:4, :]
    scale2 = mod_ref[0, 4:5, :]
    width = o_ref.shape[1]
    yn = _rms(o_ref[...].astype(_F32), gao_ref[...]).astype(_BF)
    y = _dot(yn, wo_ref[0:width, :]) + _dot(ysg_ref[...], wo_ref[width:, :])
    x1 = x_ref[...] + gate1 * y
    x1_ref[...] = x1
    h2 = (_rms(x1, g2_ref[...]) * (1.0 + scale2) + shift2).astype(_BF)
    h2_ref[...] = h2
    qp = _dot(h2, wq_ref[...]).astype(_BF)
    pad = jnp.zeros((SC_PITCH - N_KEYS, LANES), _F32)
    tiles = qp.shape[0] // LANES
    for hh in range(2 * PEER_HEADS):
        sc = _dot_nt(keys_ref[hh], qp[:, hh * HALF_DIM:(hh + 1) * HALF_DIM])
        for c in range(tiles):
            r0 = ((hh % 2) * tiles + c) * SC_PITCH
            sc_ref[0, hh // 2, r0:r0 + N_KEYS, :] = sc[:, c * LANES:(c + 1) * LANES]
            sc_ref[0, hh // 2, r0 + N_KEYS:r0 + SC_PITCH, :] = pad


def _post(o, ysg, x2d, mod3, gao, wo, g2, wq, keys, tt, s):
    t, d = x2d.shape
    width = o.shape[1]
    per_b = s // tt
    full = lambda *shape: pl.BlockSpec(shape, lambda i: (0,) * len(shape))
    return pl.pallas_call(
        _post_kernel,
        grid=(t // tt,),
        in_specs=[
            pl.BlockSpec((tt, width), lambda i: (i, 0)),
            pl.BlockSpec((tt, width), lambda i: (i, 0)),
            pl.BlockSpec((tt, d), lambda i: (i, 0)),
            pl.BlockSpec((1, 6, d), lambda i: (i // per_b, 0, 0)),
            full(1, width),
            full(2 * width, d),
            full(1, d),
            full(d, wq.shape[1]),
            full(2 * PEER_HEADS, N_KEYS, HALF_DIM),
        ],
        out_specs=[
            pl.BlockSpec((tt, d), lambda i: (i, 0)),
            pl.BlockSpec((tt, d), lambda i: (i, 0)),
            pl.BlockSpec((1, PEER_HEADS, 2 * tt // LANES * SC_PITCH, LANES), lambda i: (i, 0, 0, 0)),
        ],
        out_shape=[
            jax.ShapeDtypeStruct((t, d), _F32),
            jax.ShapeDtypeStruct((t, d), _BF),
            jax.ShapeDtypeStruct((t // tt, PEER_HEADS, 2 * tt // LANES * SC_PITCH, LANES), _F32),
        ],
        compiler_params=_cparams(("arbitrary",)),
        name="post",
    )(o, ysg, x2d, mod3, gao, wo, g2, wq, keys)


_NB = [TOPK // (a + 1) for a in range(TOPK)]


def _oddeven_merge(lo, hi, r):
    step = r * 2
    if step < hi - lo:
        yield from _oddeven_merge(lo, hi, step)
        yield from _oddeven_merge(lo + r, hi, step)
        yield from [(i, i + r) for i in range(lo + r, hi - r, step)]
    else:
        yield (lo, lo + r)


def _oddeven_merge_sort(lo, hi):
    if hi - lo >= 1:
        mid = lo + (hi - lo) // 2
        yield from _oddeven_merge_sort(lo, mid)
        yield from _oddeven_merge_sort(mid + 1, hi)
        yield from _oddeven_merge(lo, hi, 1)


_NET16 = tuple(_oddeven_merge_sort(0, TOPK - 1))

ROUTE_TILES = SUBLANES // 2
ROUTE_TOKENS = ROUTE_TILES * LANES


def _beats(vb, ib, va, ia):
    if isinstance(ia, int) and isinstance(ib, int):
        return vb > va if ib > ia else vb >= va
    return (vb > va) | ((vb == va) & (ib < ia))


def _compare_exchange(v, i, a, b):
    swap = _beats(v[b], i[b], v[a], i[a])
    v[a], v[b] = jnp.where(swap, v[b], v[a]), jnp.where(swap, v[a], v[b])
    i[a], i[b] = jnp.where(swap, i[b], i[a]), jnp.where(swap, i[a], i[b])


def _merge_top16(va, ia, vb, ib):
    cv, ci = [], []
    for r in range(TOPK):
        o = TOPK - 1 - r
        take_b = _beats(vb[o], ib[o], va[r], ia[r])
        cv.append(jnp.where(take_b, vb[o], va[r]))
        ci.append(jnp.where(take_b, ib[o], ia[r]))
    dist = TOPK // 2
    while dist:
        for r in range(TOPK):
            if not r & dist:
                _compare_exchange(cv, ci, r, r + dist)
        dist //= 2
    return cv, ci


def _sorted_groups(load_key, which):
    groups = []
    for g in which:
        v = [load_key(g * TOPK + r) for r in range(TOPK)]
        i = [g * TOPK + r for r in range(TOPK)]
        for a, b in _NET16:
            _compare_exchange(v, i, a, b)
        groups.append((v, i))
    return groups


def _top16(groups):
    while len(groups) > 1:
        groups = [_merge_top16(*groups[n], *groups[n + 1]) for n in range(0, len(groups), 2)]
    return groups[0]


def _pair_top16(v1, i1, v2, i2):
    cells = [(a, b) for a in range(TOPK) for b in range(_NB[a])]
    cand = {(a, b): v1[a] + v2[b] for a, b in cells}
    picked = []
    for r in range(TOPK):
        live = [(a, b) for a, b in cells if (a + 1) * (b + 1) <= r + 1]
        best = cand[live[0]]
        flat = jnp.zeros(best.shape, jnp.int32)
        for a, b in live[1:]:
            take = cand[a, b] > best
            best = jnp.where(take, cand[a, b], best)
            flat = jnp.where(take, a * TOPK + b, flat)
        for a, b in live:
            cand[a, b] = jnp.where(flat == a * TOPK + b, -jnp.inf, cand[a, b])
        a_sel = jnp.right_shift(flat, 4)
        b_sel = jnp.bitwise_and(flat, TOPK - 1)
        row, col = i1[r], i2[r]
        for n in range(r - 1, -1, -1):
            row = jnp.where(a_sel == n, i1[n], row)
            col = jnp.where(b_sel == n, i2[n], col)
        picked.append((best, row, col))
    return picked


def _route_sort(sc_ref, which):
    return _sorted_groups(lambda k: sc_ref[0, 0, pl.ds(k, SUBLANES, stride=SC_PITCH), :], which)


def _route_select(v, i, hd, buf, row_scr, col_scr, gate_scr):
    v2 = [pltpu.roll(x, ROUTE_TILES, axis=0) for x in v]
    i2 = [pltpu.roll(x, ROUTE_TILES, axis=0) for x in i]
    picked = _pair_top16(v, i, v2, i2)
    e = [jnp.exp(sc - picked[0][0]) for sc, _, _ in picked]
    z = e[0]
    for ek in e[1:]:
        z = z + ek
    for n, (_, row, col) in enumerate(picked):
        base = pl.multiple_of((hd * TOPK + n) * SUBLANES, SUBLANES)
        row_scr[buf, pl.ds(base, SUBLANES), :] = row
        col_scr[buf, pl.ds(base, SUBLANES), :] = col
        gate_scr[buf, pl.ds(base, SUBLANES), :] = e[n] / z


def _peer_kernel(h2_ref, sc_ref, u_ref, v_ref, x1_ref, mod_ref, gf_ref, o_ref,
                 g_scr, acc_scr, row_scr, col_scr, gate_scr, rows_t, cols_t, gates_t):
    tb = h2_ref.shape[0]
    eb = u_ref.shape[0]
    half = tb // 2
    step = pl.program_id(0)
    j = pl.program_id(1)

    @pl.when(step == 0)
    def _route_first_block():
        v, i = _top16(_route_sort(sc_ref, range(N_KEYS // TOPK)))
        _route_select(v, i, j, 0, row_scr, col_scr, gate_scr)

    @pl.when(step > 0)
    def _evaluate_and_route():
        buf = (step - 1) % 2

        @pl.when(j == 0)
        def _build_gates():
            acc_scr[...] = jnp.zeros_like(acc_scr)
            for c in range(ROUTE_TILES):
                tok = slice(c * LANES, (c + 1) * LANES)
                rows_t[tok, :] = row_scr[buf, pl.ds(c, N_SEL, stride=SUBLANES), :].T
                cols_t[tok, :] = col_scr[buf, pl.ds(c, N_SEL, stride=SUBLANES), :].T
                gates_t[tok, :] = gate_scr[buf, pl.ds(c, N_SEL, stride=SUBLANES), :].T
            key_iota = lax.broadcasted_iota(jnp.int32, (N_KEYS, N_SEL), 0)
            lane_iota = lax.broadcasted_iota(jnp.int32, (N_SEL, N_KEYS), 1)

            def gated_rows(t):
                r = rows_t[pl.ds(t, 1), :]
                g = gates_t[pl.ds(t, 1), :]
                return jnp.where(key_iota == r, g, 0.0).astype(_BF)

            def gate_matrix_nt(t):
                cidx = cols_t[pl.ds(t, 1), :]
                bmat = jnp.where(key_iota == cidx, 1.0, 0.0).astype(_BF)
                return _dot_nt(gated_rows(t), bmat)

            def gate_matrix_nn(t, col_of_slot):
                bmat = jnp.where(lane_iota == col_of_slot, 1.0, 0.0).astype(_BF)
                return _dot(gated_rows(t), bmat)

            def tile_pair(c, carry):
                ct = col_scr[buf, pl.ds(c, N_SEL, stride=SUBLANES), :]
                for t in range(LANES):
                    pp = c * LANES + t
                    base = pl.multiple_of(pp * G_PITCH, SUBLANES)
                    g_scr[pl.ds(base, N_KEYS), :] = pltpu.pack_elementwise(
                        [gate_matrix_nn(pp, ct[:, t:t + 1]), gate_matrix_nt(pp + half)],
                        packed_dtype=_BF)
                return carry

            lax.fori_loop(0, half // LANES, tile_pair, 0)

        groups = _route_sort(sc_ref, range(N_KEYS // TOPK))
        a = _dot_nt(h2_ref[...], u_ref[...])
        gparts = []
        for ri in range(eb // N_KEYS):
            words = g_scr[pl.ds(j * (eb // N_KEYS) + ri, half, stride=G_PITCH), :]
            gparts.append(jnp.concatenate(
                [pltpu.unpack_elementwise(words, index=k, packed_dtype=_BF, unpacked_dtype=_F32)
                 for k in range(2)], axis=0))
        w = (_gelu(a) * jnp.concatenate(gparts, axis=-1)).astype(_BF)
        top_v, top_i = _top16(groups)
        _route_select(top_v, top_i, j, step % 2, row_scr, col_scr, gate_scr)
        acc_scr[...] += _dot(w, v_ref[...])

        @pl.when(j == pl.num_programs(1) - 1)
        def _finish():
            gate2 = mod_ref[0, 5:6, :]
            x2 = x1_ref[...] + gate2 * acc_scr[...]
            o_ref[...] = _rms(x2, gf_ref[...])


def _peer(h2, scores, u_bf, v_bf, x1, mod3, gf, eb, s):
    t, d = h2.shape
    e = u_bf.shape[0]
    tb = ROUTE_TOKENS
    nb = t // tb
    per_b = s // tb
    assert e // eb == PEER_HEADS, "one routing head is scheduled per expert block"
    evaluated = lambda i: jnp.maximum(i - 1, 0)
    routed = lambda i: jnp.minimum(i, nb - 1)
    experts = lambda i, j: (jnp.where(i == 0, 0, j), 0)
    return pl.pallas_call(
        _peer_kernel,
        grid=(nb + 1, PEER_HEADS),
        in_specs=[
            pl.BlockSpec((tb, d), lambda i, j: (evaluated(i), 0)),
            pl.BlockSpec((1, 1) + scores.shape[2:], lambda i, j: (routed(i), j, 0, 0)),
            pl.BlockSpec((eb, d), experts),
            pl.BlockSpec((eb, d), experts),
            pl.BlockSpec((tb, d), lambda i, j: (evaluated(i), 0)),
            pl.BlockSpec((1, 6, d), lambda i, j: (evaluated(i) // per_b, 0, 0)),
            pl.BlockSpec((1, d), lambda i, j: (0, 0)),
        ],
        out_specs=pl.BlockSpec((tb, d), lambda i, j: (evaluated(i), 0)),
        out_shape=jax.ShapeDtypeStruct((t, d), _F32),
        scratch_shapes=[
            pltpu.VMEM((tb // 2 * G_PITCH, N_KEYS), jnp.uint32),
            pltpu.VMEM((tb, d), _F32),
            pltpu.VMEM((2, N_SEL * SUBLANES, LANES), jnp.int32),
            pltpu.VMEM((2, N_SEL * SUBLANES, LANES), jnp.int32),
            pltpu.VMEM((2, N_SEL * SUBLANES, LANES), _F32),
            pltpu.VMEM((tb, N_SEL), jnp.int32),
            pltpu.VMEM((tb, N_SEL), jnp.int32),
            pltpu.VMEM((tb, N_SEL), _F32),
        ],
        compiler_params=_cparams(("arbitrary", "arbitrary")),
        name="peer",
    )(h2, scores, u_bf, v_bf, x1, mod3, gf)


def _pad_cols(w, width):
    return jnp.pad(w, ((0, 0), (0, width - w.shape[1])))


def _rot_cols(w):
    half = w.shape[1] // 2
    return jnp.concatenate([-w[:, half:], w[:, :half]], axis=1)


def kernel(x, c, positions, w_ada, b_ada, g_norm1, w_in, g_q_a, w_uq, g_kv_a, w_ukv, g_sg, w_sg, b_sg,
           g_attn_out, g_sg_out, w_o, g_norm2, w_peer_q, peer_keys, peer_u, peer_v, g_final):
    b, s, d = x.shape
    depth = w_ada.shape[0]
    assert depth == 1, "the final norm is fused into the last layer's expert kernel"
    n_mod = w_ada.shape[2] // d
    t = b * s

    inv_freq = 1.0 / (ROPE_THETA ** (jnp.arange(0, ROPE, 2, dtype=_F32) / ROPE))
    invf = _pad_cols(jnp.concatenate([inv_freq, inv_freq])[None, :], LANES)
    pos3 = positions[:, :, None]
    c_pad = jnp.pad(c, ((0, SUBLANES - b), (0, 0)))

    o1 = Q_RANK
    o2 = o1 + KV_RANK
    o3 = o2 + ROPE
    xs = x
    for l in range(depth):
        mod = _adaln(c_pad, w_ada[l], b_ada[l][None, :])[:b]
        mod3 = mod.reshape(b, n_mod, d)

        w_kr = w_in[l][:, o2:o3]
        win_ext = jnp.concatenate(
            [w_in[l][:, :o2], _pad_cols(w_kr, LANES), _pad_cols(_rot_cols(w_kr), LANES), w_in[l][:, o3:]],
            axis=1).astype(_BF)
        wq3 = w_uq[l].reshape(Q_RANK, HEADS, QK_DIM)
        wq_nope = wq3[:, :, :NOPE].reshape(Q_RANK, HEADS * NOPE)
        wq_rope = [wq3[:, hd, NOPE:] for hd in range(HEADS)]
        wuq_ext = jnp.concatenate(
            [wq_nope] + [_pad_cols(w, LANES) for w in wq_rope]
            + [_pad_cols(_rot_cols(w), LANES) for w in wq_rope], axis=1).astype(_BF)

        q, k, v, ysg = _inproj(
            xs, pos3, mod3, g_norm1[l][None, :], win_ext, g_q_a[l][None, :], wuq_ext,
            g_kv_a[l][None, :], w_ukv[l].astype(_BF), invf, g_sg[l][None, :], w_sg[l].astype(_BF),
            b_sg[l][:, :, None], g_sg_out[l][None, :], ts=INPROJ_ROWS)
        o, u_bf, v_bf = _attention(q, k, v, peer_u[l], peer_v[l], tq=ATTN_Q_ROWS, tk=ATTN_K_ROWS)

        keys = peer_keys[l].reshape(2 * PEER_HEADS, N_KEYS, HALF_DIM).astype(_BF)
        x1, h2, scores = _post(
            o.reshape(t, HEADS * VDIM), ysg.reshape(t, SG_WIDTH), xs.reshape(t, d), mod3,
            g_attn_out[l][None, :], w_o[l].astype(_BF), g_norm2[l][None, :], w_peer_q[l].astype(_BF),
            keys, tt=ROUTE_TOKENS, s=s)
        out = _peer(h2, scores, u_bf, v_bf, x1, mod3,
                    g_final[None, :], eb=peer_u.shape[1] // PEER_HEADS, s=s)
        xs = out.reshape(b, s, d)
    return xs
```

```python
import functools
import math

import jax
import jax.numpy as jnp
from jax import lax
from jax.experimental import pallas as pl
from jax.experimental.pallas import tpu as pltpu

LANES = 128
SUBLANES = 8
VMEM_LIMIT_BYTES = 60 * 1024 * 1024

EPS = 1e-6
ROPE_THETA = 10000.0
HEADS = 4
NOPE = 128
ROPE = 64
VDIM = 128
QK_DIM = NOPE + ROPE
QK_PAD = 2 * LANES
Q_RANK = 256
KV_RANK = 128
SG_HEADS = 4
SG_DIM = 128
SG_CHUNK = 128
SG_WIDTH = SG_HEADS * SG_DIM
PEER_HEADS = 8
N_KEYS = 128
TOPK = 16
HALF_DIM = 128
N_SEL = PEER_HEADS * TOPK

G_PITCH = N_KEYS + SUBLANES
SC_PITCH = N_KEYS + SUBLANES

ADALN_COLS = 1024
INPROJ_ROWS = 512
ATTN_Q_ROWS = 1024
ATTN_K_ROWS = 256
ATTN_UNROLL = 8

_BF = jnp.bfloat16
_F32 = jnp.float32


def _cparams(sem):
    return pltpu.CompilerParams(dimension_semantics=sem, vmem_limit_bytes=VMEM_LIMIT_BYTES)


def _rms(x, g):
    return x * lax.rsqrt(jnp.mean(x * x, axis=-1, keepdims=True) + EPS) * g


def _gelu(x):
    return 0.5 * x * (1.0 + lax.erf(x * (1.0 / math.sqrt(2.0))))


def _dot(a, b):
    return jnp.dot(a, b, preferred_element_type=_F32)


def _dot_nt(a, b):
    return lax.dot_general(a, b, (((1,), (1,)), ((), ())), preferred_element_type=_F32)


def _adaln_kernel(c_ref, w_ref, b_ref, o_ref):
    c = c_ref[...]
    c_act = (c * jax.nn.sigmoid(c)).astype(_BF)
    o_ref[...] = _dot(c_act, w_ref[...].astype(_BF)) + b_ref[...]


def _adaln(c_pad, w_ada, b_ada):
    rows, d = c_pad.shape
    n = w_ada.shape[1]
    tn = ADALN_COLS
    return pl.pallas_call(
        _adaln_kernel,
        grid=(n // tn,),
        in_specs=[
            pl.BlockSpec((rows, d), lambda j: (0, 0)),
            pl.BlockSpec((d, tn), lambda j: (0, j)),
            pl.BlockSpec((1, tn), lambda j: (0, j)),
        ],
        out_specs=pl.BlockSpec((rows, tn), lambda j: (0, j)),
        out_shape=jax.ShapeDtypeStruct((rows, n), _F32),
        compiler_params=_cparams(("arbitrary",)),
        name="adaln",
    )(c_pad, w_ada, b_ada)


def _inproj_kernel(x_ref, pos_ref, mod_ref, g1_ref, win_ref, gq_ref, wuq_ref, gkv_ref, wukv_ref,
                   invf_ref, gsg_ref, wsg_ref, bsg_ref, gsgo_ref,
                   q_ref, k_ref, v_ref, ysg_ref, *, scale):
    ts = x_ref.shape[1]
    x = x_ref[0]
    shift1 = mod_ref[0, 0:1, :]
    scale1 = mod_ref[0, 1:2, :]
    h = _rms(x, g1_ref[...]) * (1.0 + scale1) + shift1
    z = _dot(h.astype(_BF), win_ref[...])

    o_kv = Q_RANK
    o_kr = o_kv + KV_RANK
    o_krot = o_kr + LANES
    o_u = o_krot + LANES
    o_v = o_u + SG_WIDTH

    ang = pos_ref[0].astype(_F32) * invf_ref[...]
    cos2 = jnp.cos(ang)
    sin2 = jnp.sin(ang)

    cq = _rms(z[:, :Q_RANK], gq_ref[...]).astype(_BF)
    qall = _dot(cq, wuq_ref[...])
    for hd in range(HEADS):
        q_nope = qall[:, hd * NOPE:(hd + 1) * NOPE]
        q_r = qall[:, HEADS * NOPE + hd * LANES: HEADS * NOPE + (hd + 1) * LANES]
        q_rot = qall[:, HEADS * (NOPE + LANES) + hd * LANES: HEADS * (NOPE + LANES) + (hd + 1) * LANES]
        q_ref[0, hd, :, 0:LANES] = (q_nope * scale).astype(_BF)
        q_ref[0, hd, :, LANES:QK_PAD] = ((q_r * cos2 + q_rot * sin2) * scale).astype(_BF)

    ckv = _rms(z[:, o_kv:o_kr], gkv_ref[...]).astype(_BF)
    kv = _dot(ckv, wukv_ref[...])
    k_rope = (z[:, o_kr:o_krot] * cos2 + z[:, o_krot:o_u] * sin2).astype(_BF)
    for hd in range(HEADS):
        base = hd * (NOPE + VDIM)
        k_ref[0, hd, :, 0:LANES] = kv[:, base:base + NOPE].astype(_BF)
        k_ref[0, hd, :, LANES:QK_PAD] = k_rope
        v_ref[0, hd] = kv[:, base + NOPE:base + NOPE + VDIM].astype(_BF)

    u = _gelu(z[:, o_u:o_v])
    vg = _rms(_gelu(z[:, o_v:o_v + SG_WIDTH]), gsg_ref[...]).astype(_BF)
    for ch in range(ts // SG_CHUNK):
        r0 = ch * SG_CHUNK
        parts = []
        for hd in range(SG_HEADS):
            c0 = hd * SG_DIM
            mix = _dot(wsg_ref[hd], vg[r0:r0 + SG_CHUNK, c0:c0 + SG_DIM]) + bsg_ref[hd]
            parts.append(u[r0:r0 + SG_CHUNK, c0:c0 + SG_DIM] * mix)
        y = jnp.concatenate(parts, axis=-1)
        ysg_ref[0, r0:r0 + SG_CHUNK, :] = _rms(y, gsgo_ref[...]).astype(_BF)


def _inproj(x, pos3, mod3, g1, win_ext, gq, wuq_ext, gkv, wukv, invf, gsg, wsg, bsg3, gsgo, ts):
    b, s, d = x.shape
    nz = win_ext.shape[1]
    full = lambda *shape: pl.BlockSpec(shape, lambda bi, i: (0,) * len(shape))
    return pl.pallas_call(
        functools.partial(_inproj_kernel, scale=QK_DIM ** -0.5 * math.log2(math.e)),
        grid=(b, s // ts),
        in_specs=[
            pl.BlockSpec((1, ts, d), lambda bi, i: (bi, i, 0)),
            pl.BlockSpec((1, ts, 1), lambda bi, i: (bi, i, 0)),
            pl.BlockSpec((1, 6, d), lambda bi, i: (bi, 0, 0)),
            full(1, d),
            full(d, nz),
            full(1, Q_RANK),
            full(Q_RANK, wuq_ext.shape[1]),
            full(1, KV_RANK),
            full(KV_RANK, wukv.shape[1]),
            full(1, LANES),
            full(1, SG_WIDTH),
            full(SG_HEADS, SG_CHUNK, SG_CHUNK),
            full(SG_HEADS, SG_CHUNK, 1),
            full(1, SG_WIDTH),
        ],
        out_specs=[
            pl.BlockSpec((1, HEADS, ts, QK_PAD), lambda bi, i: (bi, 0, i, 0)),
            pl.BlockSpec((1, HEADS, ts, QK_PAD), lambda bi, i: (bi, 0, i, 0)),
            pl.BlockSpec((1, HEADS, ts, VDIM), lambda bi, i: (bi, 0, i, 0)),
            pl.BlockSpec((1, ts, SG_WIDTH), lambda bi, i: (bi, i, 0)),
        ],
        out_shape=[
            jax.ShapeDtypeStruct((b, HEADS, s, QK_PAD), _BF),
            jax.ShapeDtypeStruct((b, HEADS, s, QK_PAD), _BF),
            jax.ShapeDtypeStruct((b, HEADS, s, VDIM), _BF),
            jax.ShapeDtypeStruct((b, s, SG_WIDTH), _BF),
        ],
        compiler_params=_cparams(("arbitrary", "arbitrary")),
        name="inproj",
    )(x, pos3, mod3, g1, win_ext, gq, wuq_ext, gkv, wukv, invf, gsg, wsg, bsg3, gsgo)


def _attn_kernel(q_ref, k_ref, v_ref, uf_ref, vf_ref, o_ref, ubf_ref, vbf_ref, s_scr, *, unroll):
    nk, tq, tk = s_scr.shape
    q = q_ref[0, 0]
    ones = jnp.ones((tk, LANES), _BF)

    def score_chunk(j, mrun):
        sc = _dot_nt(q, k_ref[0, 0, pl.ds(pl.multiple_of(j * tk, tk), tk), :])
        s_scr[j] = sc
        for c in range(tk // LANES):
            mrun = jnp.maximum(mrun, sc[:, c * LANES:(c + 1) * LANES])
        return mrun

    mrun = lax.fori_loop(0, nk, score_chunk, jnp.full((tq, LANES), -jnp.inf, _F32), unroll=unroll)
    m = jnp.max(mrun, axis=-1, keepdims=True)

    def value_chunk(j, acc):
        p = jnp.exp2(s_scr[j] - m).astype(_BF)
        vc = v_ref[0, 0, pl.ds(pl.multiple_of(j * tk, tk), tk), :]
        return acc + _dot(p, jnp.concatenate([vc, ones], axis=-1))

    acc = lax.fori_loop(0, nk, value_chunk, jnp.zeros((tq, VDIM + LANES), _F32), unroll=unroll)
    o_ref[0] = (acc[:, :VDIM] / acc[:, VDIM:]).astype(_BF)
    ubf_ref[...] = uf_ref[...].astype(_BF)
    vbf_ref[...] = vf_ref[...].astype(_BF)


def _attention(q, k, v, u_f32, v_f32, tq, tk):
    b, h, s, _ = q.shape
    nq = s // tq
    e, d = u_f32.shape
    er = e // (b * h * nq)
    step = lambda bi, hi, i: ((bi * h + hi) * nq + i, 0)
    return pl.pallas_call(
        functools.partial(_attn_kernel, unroll=ATTN_UNROLL),
        grid=(b, h, nq),
        scratch_shapes=[pltpu.VMEM((s // tk, tq, tk), _F32)],
        in_specs=[
            pl.BlockSpec((1, 1, tq, QK_PAD), lambda bi, hi, i: (bi, hi, i, 0)),
            pl.BlockSpec((1, 1, s, QK_PAD), lambda bi, hi, i: (bi, hi, 0, 0)),
            pl.BlockSpec((1, 1, s, VDIM), lambda bi, hi, i: (bi, hi, 0, 0)),
            pl.BlockSpec((er, d), step),
            pl.BlockSpec((er, d), step),
        ],
        out_specs=[
            pl.BlockSpec((1, tq, VDIM), lambda bi, hi, i: (bi, i, hi)),
            pl.BlockSpec((er, d), step),
            pl.BlockSpec((er, d), step),
        ],
        out_shape=[
            jax.ShapeDtypeStruct((b, s, h * VDIM), _BF),
            jax.ShapeDtypeStruct((e, d), _BF),
            jax.ShapeDtypeStruct((e, d), _BF),
        ],
        compiler_params=_cparams(("arbitrary", "arbitrary", "arbitrary")),
        name="attention",
    )(q, k, v, u_f32, v_f32)


def _post_kernel(o_ref, ysg_ref, x_ref, mod_ref, gao_ref, wo_ref, g2_ref, wq_ref, keys_ref,
                 x1_ref, h2_ref, sc_ref):
    gate1 = mod_ref[0, 2:3, :]
    shift2 = mod_ref[0, 3:4, :]
    scale2 = mod_ref[0, 4:5, :]
    width = o_ref.shape[1]
    yn = _rms(o_ref[...].astype(_F32), gao_ref[...]).astype(_BF)
    y = _dot(yn, wo_ref[0:width, :]) + _dot(ysg_ref[...], wo_ref[width:, :])
    x1 = x_ref[...] + gate1 * y
    x1_ref[...] = x1
    h2 = (_rms(x1, g2_ref[...]) * (1.0 + scale2) + shift2).astype(_BF)
    h2_ref[...] = h2
    qp = _dot(h2, wq_ref[...]).astype(_BF)
    pad = jnp.zeros((SC_PITCH - N_KEYS, LANES), _F32)
    tiles = qp.shape[0] // LANES
    for hh in range(2 * PEER_HEADS):
        sc = _dot_nt(keys_ref[hh], qp[:, hh * HALF_DIM:(hh + 1) * HALF_DIM])
        for c in range(tiles):
            r0 = ((hh % 2) * tiles + c) * SC_PITCH
            sc_ref[0, hh // 2, r0:r0 + N_KEYS, :] = sc[:, c * LANES:(c + 1) * LANES]
            sc_ref[0, hh // 2, r0 + N_KEYS:r0 + SC_PITCH, :] = pad


def _post(o, ysg, x2d, mod3, gao, wo, g2, wq, keys, tt, s):
    t, d = x2d.shape
    width = o.shape[1]
    per_b = s // tt
    full = lambda *shape: pl.BlockSpec(shape, lambda i: (0,) * len(shape))
    return pl.pallas_call(
        _post_kernel,
        grid=(t // tt,),
        in_specs=[
            pl.BlockSpec((tt, width), lambda i: (i, 0)),
            pl.BlockSpec((tt, width), lambda i: (i, 0)),
            pl.BlockSpec((tt, d), lambda i: (i, 0)),
            pl.BlockSpec((1, 6, d), lambda i: (i // per_b, 0, 0)),
            full(1, width),
            full(2 * width, d),
            full(1, d),
            full(d, wq.shape[1]),
            full(2 * PEER_HEADS, N_KEYS, HALF_DIM),
        ],
        out_specs=[
            pl.BlockSpec((tt, d), lambda i: (i, 0)),
            pl.BlockSpec((tt, d), lambda i: (i, 0)),
            pl.BlockSpec((1, PEER_HEADS, 2 * tt // LANES * SC_PITCH, LANES), lambda i: (i, 0, 0, 0)),
        ],
        out_shape=[
            jax.ShapeDtypeStruct((t, d), _F32),
            jax.ShapeDtypeStruct((t, d), _BF),
            jax.ShapeDtypeStruct((t // tt, PEER_HEADS, 2 * tt // LANES * SC_PITCH, LANES), _F32),
        ],
        compiler_params=_cparams(("arbitrary",)),
        name="post",
    )(o, ysg, x2d, mod3, gao, wo, g2, wq, keys)


_NB = [TOPK // (a + 1) for a in range(TOPK)]


def _oddeven_merge(lo, hi, r):
    step = r * 2
    if step < hi - lo:
        yield from _oddeven_merge(lo, hi, step)
        yield from _oddeven_merge(lo + r, hi, step)
        yield from [(i, i + r) for i in range(lo + r, hi - r, step)]
    else:
        yield (lo, lo + r)


def _oddeven_merge_sort(lo, hi):
    if hi - lo >= 1:
        mid = lo + (hi - lo) // 2
        yield from _oddeven_merge_sort(lo, mid)
        yield from _oddeven_merge_sort(mid + 1, hi)
        yield from _oddeven_merge(lo, hi, 1)


_NET16 = tuple(_oddeven_merge_sort(0, TOPK - 1))

ROUTE_TILES = SUBLANES // 2
ROUTE_TOKENS = ROUTE_TILES * LANES


def _beats(vb, ib, va, ia):
    if isinstance(ia, int) and isinstance(ib, int):
        return vb > va if ib > ia else vb >= va
    return (vb > va) | ((vb == va) & (ib < ia))


def _compare_exchange(v, i, a, b):
    swap = _beats(v[b], i[b], v[a], i[a])
    v[a], v[b] = jnp.where(swap, v[b], v[a]), jnp.where(swap, v[a], v[b])
    i[a], i[b] = jnp.where(swap, i[b], i[a]), jnp.where(swap, i[a], i[b])


def _merge_top16(va, ia, vb, ib):
    cv, ci = [], []
    for r in range(TOPK):
        o = TOPK - 1 - r
        take_b = _beats(vb[o], ib[o], va[r], ia[r])
        cv.append(jnp.where(take_b, vb[o], va[r]))
        ci.append(jnp.where(take_b, ib[o], ia[r]))
    dist = TOPK // 2
    while dist:
        for r in range(TOPK):
            if not r & dist:
                _compare_exchange(cv, ci, r, r + dist)
        dist //= 2
    return cv, ci


def _sorted_groups(load_key, which):
    groups = []
    for g in which:
        v = [load_key(g * TOPK + r) for r in range(TOPK)]
        i = [g * TOPK + r for r in range(TOPK)]
        for a, b in _NET16:
            _compare_exchange(v, i, a, b)
        groups.append((v, i))
    return groups


def _top16(groups):
    while len(groups) > 1:
        groups = [_merge_top16(*groups[n], *groups[n + 1]) for n in range(0, len(groups), 2)]
    return groups[0]


def _pair_top16(v1, i1, v2, i2):
    cells = [(a, b) for a in range(TOPK) for b in range(_NB[a])]
    cand = {(a, b): v1[a] + v2[b] for a, b in cells}
    picked = []
    for r in range(TOPK):
        live = [(a, b) for a, b in cells if (a + 1) * (b + 1) <= r + 1]
        best = cand[live[0]]
        flat = jnp.zeros(best.shape, jnp.int32)
        for a, b in live[1:]:
            take = cand[a, b] > best
            best = jnp.where(take, cand[a, b], best)
            flat = jnp.where(take, a * TOPK + b, flat)
        for a, b in live:
            cand[a, b] = jnp.where(flat == a * TOPK + b, -jnp.inf, cand[a, b])
        a_sel = jnp.right_shift(flat, 4)
        b_sel = jnp.bitwise_and(flat, TOPK - 1)
        row, col = i1[r], i2[r]
        for n in range(r - 1, -1, -1):
            row = jnp.where(a_sel == n, i1[n], row)
            col = jnp.where(b_sel == n, i2[n], col)
        picked.append((best, row, col))
    return picked


def _route_sort(sc_ref, which):
    return _sorted_groups(lambda k: sc_ref[0, 0, pl.ds(k, SUBLANES, stride=SC_PITCH), :], which)


def _route_select(v, i, hd, buf, row_scr, col_scr, gate_scr):
    v2 = [pltpu.roll(x, ROUTE_TILES, axis=0) for x in v]
    i2 = [pltpu.roll(x, ROUTE_TILES, axis=0) for x in i]
    picked = _pair_top16(v, i, v2, i2)
    e = [jnp.exp(sc - picked[0][0]) for sc, _, _ in picked]
    z = e[0]
    for ek in e[1:]:
        z = z + ek
    for n, (_, row, col) in enumerate(picked):
        base = pl.multiple_of((hd * TOPK + n) * SUBLANES, SUBLANES)
        row_scr[buf, pl.ds(base, SUBLANES), :] = row
        col_scr[buf, pl.ds(base, SUBLANES), :] = col
        gate_scr[buf, pl.ds(base, SUBLANES), :] = e[n] / z


def _peer_kernel(h2_ref, sc_ref, u_ref, v_ref, x1_ref, mod_ref, gf_ref, o_ref,
                 g_scr, acc_scr, row_scr, col_scr, gate_scr, rows_t, gates_t):
    tb = h2_ref.shape[0]
    eb = u_ref.shape[0]
    half = tb // 2
    step = pl.program_id(0)
    j = pl.program_id(1)

    @pl.when(step == 0)
    def _route_first_block():
        v, i = _top16(_route_sort(sc_ref, range(N_KEYS // TOPK)))
        _route_select(v, i, j, 0, row_scr, col_scr, gate_scr)

    @pl.when(step > 0)
    def _evaluate_and_route():
        buf = (step - 1) % 2

        @pl.when(j == 0)
        def _build_gates():
            acc_scr[...] = jnp.zeros_like(acc_scr)
            for c in range(ROUTE_TILES):
                tok = slice(c * LANES, (c + 1) * LANES)
                rows_t[tok, :] = row_scr[buf, pl.ds(c, N_SEL, stride=SUBLANES), :].T
                gates_t[tok, :] = gate_scr[buf, pl.ds(c, N_SEL, stride=SUBLANES), :].T
            key_iota = lax.broadcasted_iota(jnp.int32, (N_KEYS, N_SEL), 0)
            lane_iota = lax.broadcasted_iota(jnp.int32, (N_SEL, N_KEYS), 1).astype(_BF)
            one, zero = jnp.ones((), _BF), jnp.zeros((), _BF)

            def gate_matrix(t, col_of_slot):
                r = rows_t[pl.ds(t, 1), :]
                g = gates_t[pl.ds(t, 1), :]
                a = jnp.where(key_iota == r, g, 0.0).astype(_BF)
                bmat = jnp.where(lane_iota == col_of_slot, one, zero)
                return _dot(a, bmat)

            def column_ids(c):
                return col_scr[buf, pl.ds(c, N_SEL, stride=SUBLANES), :].astype(_F32).astype(_BF)

            def tile_pair(c, carry):
                lo = column_ids(c)
                hi = column_ids(c + half // LANES)
                for t in range(LANES):
                    pp = c * LANES + t
                    base = pl.multiple_of(pp * G_PITCH, SUBLANES)
                    g_scr[pl.ds(base, N_KEYS), :] = pltpu.pack_elementwise(
                        [gate_matrix(pp, lo[:, t:t + 1]), gate_matrix(pp + half, hi[:, t:t + 1])],
                        packed_dtype=_BF)
                return carry

            lax.fori_loop(0, half // LANES, tile_pair, 0)

        groups = _route_sort(sc_ref, range(N_KEYS // TOPK))
        a = _dot_nt(h2_ref[...], u_ref[...])
        gparts = []
        for ri in range(eb // N_KEYS):
            words = g_scr[pl.ds(j * (eb // N_KEYS) + ri, half, stride=G_PITCH), :]
            gparts.append(jnp.concatenate(
                [pltpu.unpack_elementwise(words, index=k, packed_dtype=_BF, unpacked_dtype=_F32)
                 for k in range(2)], axis=0))
        w = (_gelu(a) * jnp.concatenate(gparts, axis=-1)).astype(_BF)
        top_v, top_i = _top16(groups)
        _route_select(top_v, top_i, j, step % 2, row_scr, col_scr, gate_scr)
        acc_scr[...] += _dot(w, v_ref[...])

        @pl.when(j == pl.num_programs(1) - 1)
        def _finish():
            gate2 = mod_ref[0, 5:6, :]
            x2 = x1_ref[...] + gate2 * acc_scr[...]
            o_ref[...] = _rms(x2, gf_ref[...])


def _peer(h2, scores, u_bf, v_bf, x1, mod3, gf, eb, s):
    t, d = h2.shape
    e = u_bf.shape[0]
    tb = ROUTE_TOKENS
    nb = t // tb
    per_b = s // tb
    assert e // eb == PEER_HEADS, "one routing head is scheduled per expert block"
    evaluated = lambda i: jnp.maximum(i - 1, 0)
    routed = lambda i: jnp.minimum(i, nb - 1)
    experts = lambda i, j: (jnp.where(i == 0, 0, j), 0)
    return pl.pallas_call(
        _peer_kernel,
        grid=(nb + 1, PEER_HEADS),
        in_specs=[
            pl.BlockSpec((tb, d), lambda i, j: (evaluated(i), 0)),
            pl.BlockSpec((1, 1) + scores.shape[2:], lambda i, j: (routed(i), j, 0, 0)),
            pl.BlockSpec((eb, d), experts),
            pl.BlockSpec((eb, d), experts),
            pl.BlockSpec((tb, d), lambda i, j: (evaluated(i), 0)),
            pl.BlockSpec((1, 6, d), lambda i, j: (evaluated(i) // per_b, 0, 0)),
            pl.BlockSpec((1, d), lambda i, j: (0, 0)),
        ],
        out_specs=pl.BlockSpec((tb, d), lambda i, j: (evaluated(i), 0)),
        out_shape=jax.ShapeDtypeStruct((t, d), _F32),
        scratch_shapes=[
            pltpu.VMEM((tb // 2 * G_PITCH, N_KEYS), jnp.uint32),
            pltpu.VMEM((tb, d), _F32),
            pltpu.VMEM((2, N_SEL * SUBLANES, LANES), jnp.int32),
            pltpu.VMEM((2, N_SEL * SUBLANES, LANES), jnp.int32),
            pltpu.VMEM((2, N_SEL * SUBLANES, LANES), _F32),
            pltpu.VMEM((tb, N_SEL), jnp.int32),
            pltpu.VMEM((tb, N_SEL), _F32),
        ],
        compiler_params=_cparams(("arbitrary", "arbitrary")),
        name="peer",
    )(h2, scores, u_bf, v_bf, x1, mod3, gf)


def _pad_cols(w, width):
    return jnp.pad(w, ((0, 0), (0, width - w.shape[1])))


def _rot_cols(w):
    half = w.shape[1] // 2
    return jnp.concatenate([-w[:, half:], w[:, :half]], axis=1)


def kernel(x, c, positions, w_ada, b_ada, g_norm1, w_in, g_q_a, w_uq, g_kv_a, w_ukv, g_sg, w_sg, b_sg,
           g_attn_out, g_sg_out, w_o, g_norm2, w_peer_q, peer_keys, peer_u, peer_v, g_final):
    b, s, d = x.shape
    depth = w_ada.shape[0]
    assert depth == 1, "the final norm is fused into the last layer's expert kernel"
    n_mod = w_ada.shape[2] // d
    t = b * s

    inv_freq = 1.0 / (ROPE_THETA ** (jnp.arange(0, ROPE, 2, dtype=_F32) / ROPE))
    invf = _pad_cols(jnp.concatenate([inv_freq, inv_freq])[None, :], LANES)
    pos3 = positions[:, :, None]
    c_pad = jnp.pad(c, ((0, SUBLANES - b), (0, 0)))

    o1 = Q_RANK
    o2 = o1 + KV_RANK
    o3 = o2 + ROPE
    xs = x
    for l in range(depth):
        mod = _adaln(c_pad, w_ada[l], b_ada[l][None, :])[:b]
        mod3 = mod.reshape(b, n_mod, d)

        w_kr = w_in[l][:, o2:o3]
        win_ext = jnp.concatenate(
            [w_in[l][:, :o2], _pad_cols(w_kr, LANES), _pad_cols(_rot_cols(w_kr), LANES), w_in[l][:, o3:]],
            axis=1).astype(_BF)
        wq3 = w_uq[l].reshape(Q_RANK, HEADS, QK_DIM)
        wq_nope = wq3[:, :, :NOPE].reshape(Q_RANK, HEADS * NOPE)
        wq_rope = [wq3[:, hd, NOPE:] for hd in range(HEADS)]
        wuq_ext = jnp.concatenate(
            [wq_nope] + [_pad_cols(w, LANES) for w in wq_rope]
            + [_pad_cols(_rot_cols(w), LANES) for w in wq_rope], axis=1).astype(_BF)

        q, k, v, ysg = _inproj(
            xs, pos3, mod3, g_norm1[l][None, :], win_ext, g_q_a[l][None, :], wuq_ext,
            g_kv_a[l][None, :], w_ukv[l].astype(_BF), invf, g_sg[l][None, :], w_sg[l].astype(_BF),
            b_sg[l][:, :, None], g_sg_out[l][None, :], ts=INPROJ_ROWS)
        o, u_bf, v_bf = _attention(q, k, v, peer_u[l], peer_v[l], tq=ATTN_Q_ROWS, tk=ATTN_K_ROWS)

        keys = peer_keys[l].reshape(2 * PEER_HEADS, N_KEYS, HALF_DIM).astype(_BF)
        x1, h2, scores = _post(
            o.reshape(t, HEADS * VDIM), ysg.reshape(t, SG_WIDTH), xs.reshape(t, d), mod3,
            g_attn_out[l][None, :], w_o[l].astype(_BF), g_norm2[l][None, :], w_peer_q[l].astype(_BF),
            keys, tt=ROUTE_TOKENS, s=s)
        out = _peer(h2, scores, u_bf, v_bf, x1, mod3,
                    g_final[None, :], eb=peer_u.shape[1] // PEER_HEADS, s=s)
        xs = out.reshape(b, s, d)
    return xs
```

```python
import functools
import math

import jax
import jax.numpy as jnp
from jax import lax
from jax.experimental import pallas as pl
from jax.experimental.pallas import tpu as pltpu

LANES = 128
SUBLANES = 8
VMEM_LIMIT_BYTES = 60 * 1024 * 1024

EPS = 1e-6
ROPE_THETA = 10000.0
HEADS = 4
NOPE = 128
ROPE = 64
VDIM = 128
QK_DIM = NOPE + ROPE
QK_PAD = 2 * LANES
Q_RANK = 256
KV_RANK = 128
SG_HEADS = 4
SG_DIM = 128
SG_CHUNK = 128
SG_WIDTH = SG_HEADS * SG_DIM
PEER_HEADS = 8
N_KEYS = 128
TOPK = 16
HALF_DIM = 128
N_SEL = PEER_HEADS * TOPK

G_PITCH = N_KEYS + SUBLANES
SC_PITCH = N_KEYS + SUBLANES

ADALN_COLS = 1024
INPROJ_ROWS = 512
ATTN_Q_ROWS = 1024
ATTN_K_ROWS = 256
ATTN_UNROLL = 8

_BF = jnp.bfloat16
_F32 = jnp.float32


def _cparams(sem):
    return pltpu.CompilerParams(dimension_semantics=sem, vmem_limit_bytes=VMEM_LIMIT_BYTES)


def _rms(x, g):
    return x * lax.rsqrt(jnp.mean(x * x, axis=-1, keepdims=True) + EPS) * g


def _gelu(x):
    return 0.5 * x * (1.0 + lax.erf(x * (1.0 / math.sqrt(2.0))))


def _dot(a, b):
    return jnp.dot(a, b, preferred_element_type=_F32)


def _dot_nt(a, b):
    return lax.dot_general(a, b, (((1,), (1,)), ((), ())), preferred_element_type=_F32)


def _adaln_kernel(c_ref, w_ref, b_ref, o_ref):
    c = c_ref[...]
    c_act = (c * jax.nn.sigmoid(c)).astype(_BF)
    o_ref[...] = _dot(c_act, w_ref[...].astype(_BF)) + b_ref[...]


def _adaln(c_pad, w_ada, b_ada):
    rows, d = c_pad.shape
    n = w_ada.shape[1]
    tn = ADALN_COLS
    return pl.pallas_call(
        _adaln_kernel,
        grid=(n // tn,),
        in_specs=[
            pl.BlockSpec((rows, d), lambda j: (0, 0)),
            pl.BlockSpec((d, tn), lambda j: (0, j)),
            pl.BlockSpec((1, tn), lambda j: (0, j)),
        ],
        out_specs=pl.BlockSpec((rows, tn), lambda j: (0, j)),
        out_shape=jax.ShapeDtypeStruct((rows, n), _F32),
        compiler_params=_cparams(("arbitrary",)),
        name="adaln",
    )(c_pad, w_ada, b_ada)


def _inproj_kernel(x_ref, pos_ref, mod_ref, g1_ref, win_ref, gq_ref, wuq_ref, gkv_ref, wukv_ref,
                   invf_ref, gsg_ref, wsg_ref, bsg_ref, gsgo_ref,
                   q_ref, k_ref, v_ref, ysg_ref, *, scale):
    ts = x_ref.shape[1]
    x = x_ref[0]
    shift1 = mod_ref[0, 0:1, :]
    scale1 = mod_ref[0, 1:2, :]
    h = _rms(x, g1_ref[...]) * (1.0 + scale1) + shift1
    z = _dot(h.astype(_BF), win_ref[...])

    o_kv = Q_RANK
    o_kr = o_kv + KV_RANK
    o_krot = o_kr + LANES
    o_u = o_krot + LANES
    o_v = o_u + SG_WIDTH

    ang = pos_ref[0].astype(_F32) * invf_ref[...]
    cos2 = jnp.cos(ang)
    sin2 = jnp.sin(ang)

    cq = _rms(z[:, :Q_RANK], gq_ref[...]).astype(_BF)
    qall = _dot(cq, wuq_ref[...])
    for hd in range(HEADS):
        q_nope = qall[:, hd * NOPE:(hd + 1) * NOPE]
        q_r = qall[:, HEADS * NOPE + hd * LANES: HEADS * NOPE + (hd + 1) * LANES]
        q_rot = qall[:, HEADS * (NOPE + LANES) + hd * LANES: HEADS * (NOPE + LANES) + (hd + 1) * LANES]
        q_ref[0, hd, :, 0:LANES] = (q_nope * scale).astype(_BF)
        q_ref[0, hd, :, LANES:QK_PAD] = ((q_r * cos2 + q_rot * sin2) * scale).astype(_BF)

    ckv = _rms(z[:, o_kv:o_kr], gkv_ref[...]).astype(_BF)
    kv = _dot(ckv, wukv_ref[...])
    k_rope = (z[:, o_kr:o_krot] * cos2 + z[:, o_krot:o_u] * sin2).astype(_BF)
    for hd in range(HEADS):
        base = hd * (NOPE + VDIM)
        k_ref[0, hd, :, 0:LANES] = kv[:, base:base + NOPE].astype(_BF)
        k_ref[0, hd, :, LANES:QK_PAD] = k_rope
        v_ref[0, hd] = kv[:, base + NOPE:base + NOPE + VDIM].astype(_BF)

    u = _gelu(z[:, o_u:o_v])
    vg = _rms(_gelu(z[:, o_v:o_v + SG_WIDTH]), gsg_ref[...]).astype(_BF)
    for ch in range(ts // SG_CHUNK):
        r0 = ch * SG_CHUNK
        parts = []
        for hd in range(SG_HEADS):
            c0 = hd * SG_DIM
            mix = _dot(wsg_ref[hd], vg[r0:r0 + SG_CHUNK, c0:c0 + SG_DIM]) + bsg_ref[hd]
            parts.append(u[r0:r0 + SG_CHUNK, c0:c0 + SG_DIM] * mix)
        y = jnp.concatenate(parts, axis=-1)
        ysg_ref[0, r0:r0 + SG_CHUNK, :] = _rms(y, gsgo_ref[...]).astype(_BF)


def _inproj(x, pos3, mod3, g1, win_ext, gq, wuq_ext, gkv, wukv, invf, gsg, wsg, bsg3, gsgo, ts):
    b, s, d = x.shape
    nz = win_ext.shape[1]
    full = lambda *shape: pl.BlockSpec(shape, lambda bi, i: (0,) * len(shape))
    return pl.pallas_call(
        functools.partial(_inproj_kernel, scale=QK_DIM ** -0.5 * math.log2(math.e)),
        grid=(b, s // ts),
        in_specs=[
            pl.BlockSpec((1, ts, d), lambda bi, i: (bi, i, 0)),
            pl.BlockSpec((1, ts, 1), lambda bi, i: (bi, i, 0)),
            pl.BlockSpec((1, 6, d), lambda bi, i: (bi, 0, 0)),
            full(1, d),
            full(d, nz),
            full(1, Q_RANK),
            full(Q_RANK, wuq_ext.shape[1]),
            full(1, KV_RANK),
            full(KV_RANK, wukv.shape[1]),
            full(1, LANES),
            full(1, SG_WIDTH),
            full(SG_HEADS, SG_CHUNK, SG_CHUNK),
            full(SG_HEADS, SG_CHUNK, 1),
            full(1, SG_WIDTH),
        ],
        out_specs=[
            pl.BlockSpec((1, HEADS, ts, QK_PAD), lambda bi, i: (bi, 0, i, 0)),
            pl.BlockSpec((1, HEADS, ts, QK_PAD), lambda bi, i: (bi, 0, i, 0)),
            pl.BlockSpec((1, HEADS, ts, VDIM), lambda bi, i: (bi, 0, i, 0)),
            pl.BlockSpec((1, ts, SG_WIDTH), lambda bi, i: (bi, i, 0)),
        ],
        out_shape=[
            jax.ShapeDtypeStruct((b, HEADS, s, QK_PAD), _BF),
            jax.ShapeDtypeStruct((b, HEADS, s, QK_PAD), _BF),
            jax.ShapeDtypeStruct((b, HEADS, s, VDIM), _BF),
            jax.ShapeDtypeStruct((b, s, SG_WIDTH), _BF),
        ],
        compiler_params=_cparams(("arbitrary", "arbitrary")),
        name="inproj",
    )(x, pos3, mod3, g1, win_ext, gq, wuq_ext, gkv, wukv, invf, gsg, wsg, bsg3, gsgo)


def _attn_kernel(q_ref, k_ref, v_ref, uf_ref, vf_ref, o_ref, ubf_ref, vbf_ref, s_scr, *, unroll):
    nk, tq, tk = s_scr.shape
    q = q_ref[0, 0]
    ones = jnp.ones((tk, LANES), _BF)

    def score_chunk(j, mrun):
        sc = _dot_nt(q, k_ref[0, 0, pl.ds(pl.multiple_of(j * tk, tk), tk), :])
        s_scr[j] = sc
        for c in range(tk // LANES):
            mrun = jnp.maximum(mrun, sc[:, c * LANES:(c + 1) * LANES])
        return mrun

    mrun = lax.fori_loop(0, nk, score_chunk, jnp.full((tq, LANES), -jnp.inf, _F32), unroll=unroll)
    m = jnp.max(mrun, axis=-1, keepdims=True)

    def value_chunk(j, acc):
        p = jnp.exp2(s_scr[j] - m).astype(_BF)
        vc = v_ref[0, 0, pl.ds(pl.multiple_of(j * tk, tk), tk), :]
        return acc + _dot(p, jnp.concatenate([vc, ones], axis=-1))

    acc = lax.fori_loop(0, nk, value_chunk, jnp.zeros((tq, VDIM + LANES), _F32), unroll=unroll)
    o_ref[0] = (acc[:, :VDIM] / acc[:, VDIM:]).astype(_BF)
    ubf_ref[...] = uf_ref[...].astype(_BF)
    vbf_ref[...] = vf_ref[...].astype(_BF)


def _attention(q, k, v, u_f32, v_f32, tq, tk):
    b, h, s, _ = q.shape
    nq = s // tq
    e, d = u_f32.shape
    er = e // (b * h * nq)
    step = lambda bi, hi, i: ((bi * h + hi) * nq + i, 0)
    return pl.pallas_call(
        functools.partial(_attn_kernel, unroll=ATTN_UNROLL),
        grid=(b, h, nq),
        scratch_shapes=[pltpu.VMEM((s // tk, tq, tk), _F32)],
        in_specs=[
            pl.BlockSpec((1, 1, tq, QK_PAD), lambda bi, hi, i: (bi, hi, i, 0)),
            pl.BlockSpec((1, 1, s, QK_PAD), lambda bi, hi, i: (bi, hi, 0, 0)),
            pl.BlockSpec((1, 1, s, VDIM), lambda bi, hi, i: (bi, hi, 0, 0)),
            pl.BlockSpec((er, d), step),
            pl.BlockSpec((er, d), step),
        ],
        out_specs=[
            pl.BlockSpec((1, tq, VDIM), lambda bi, hi, i: (bi, i, hi)),
            pl.BlockSpec((er, d), step),
            pl.BlockSpec((er, d), step),
        ],
        out_shape=[
            jax.ShapeDtypeStruct((b, s, h * VDIM), _BF),
            jax.ShapeDtypeStruct((e, d), _BF),
            jax.ShapeDtypeStruct((e, d), _BF),
        ],
        compiler_params=_cparams(("arbitrary", "arbitrary", "arbitrary")),
        name="attention",
    )(q, k, v, u_f32, v_f32)


def _post_kernel(o_ref, ysg_ref, x_ref, mod_ref, gao_ref, wo_ref, g2_ref, wq_ref, keys_ref,
                 x1_ref, h2_ref, sc_ref):
    gate1 = mod_ref[0, 2:3, :]
    shift2 = mod_ref[0, 3:4, :]
    scale2 = mod_ref[0, 4:5, :]
    width = o_ref.shape[1]
    yn = _rms(o_ref[...].astype(_F32), gao_ref[...]).astype(_BF)
    y = _dot(yn, wo_ref[0:width, :]) + _dot(ysg_ref[...], wo_ref[width:, :])
    x1 = x_ref[...] + gate1 * y
    x1_ref[...] = x1
    h2 = (_rms(x1, g2_ref[...]) * (1.0 + scale2) + shift2).astype(_BF)
    h2_ref[...] = h2
    qp = _dot(h2, wq_ref[...]).astype(_BF)
    pad = jnp.zeros((SC_PITCH - N_KEYS, LANES), _F32)
    tiles = qp.shape[0] // LANES
    for hh in range(2 * PEER_HEADS):
        sc = _dot_nt(keys_ref[hh], qp[:, hh * HALF_DIM:(hh + 1) * HALF_DIM])
        for c in range(tiles):
            r0 = ((hh % 2) * tiles + c) * SC_PITCH
            sc_ref[0, hh // 2, r0:r0 + N_KEYS, :] = sc[:, c * LANES:(c + 1) * LANES]
            sc_ref[0, hh // 2, r0 + N_KEYS:r0 + SC_PITCH, :] = pad


def _post(o, ysg, x2d, mod3, gao, wo, g2, wq, keys, tt, s):
    t, d = x2d.shape
    width = o.shape[1]
    per_b = s // tt
    full = lambda *shape: pl.BlockSpec(shape, lambda i: (0,) * len(shape))
    return pl.pallas_call(
        _post_kernel,
        grid=(t // tt,),
        in_specs=[
            pl.BlockSpec((tt, width), lambda i: (i, 0)),
            pl.BlockSpec((tt, width), lambda i: (i, 0)),
            pl.BlockSpec((tt, d), lambda i: (i, 0)),
            pl.BlockSpec((1, 6, d), lambda i: (i // per_b, 0, 0)),
            full(1, width),
            full(2 * width, d),
            full(1, d),
            full(d, wq.shape[1]),
            full(2 * PEER_HEADS, N_KEYS, HALF_DIM),
        ],
        out_specs=[
            pl.BlockSpec((tt, d), lambda i: (i, 0)),
            pl.BlockSpec((tt, d), lambda i: (i, 0)),
            pl.BlockSpec((1, PEER_HEADS, 2 * tt // LANES * SC_PITCH, LANES), lambda i: (i, 0, 0, 0)),
        ],
        out_shape=[
            jax.ShapeDtypeStruct((t, d), _F32),
            jax.ShapeDtypeStruct((t, d), _BF),
            jax.ShapeDtypeStruct((t // tt, PEER_HEADS, 2 * tt // LANES * SC_PITCH, LANES), _F32),
        ],
        compiler_params=_cparams(("arbitrary",)),
        name="post",
    )(o, ysg, x2d, mod3, gao, wo, g2, wq, keys)


_NB = [TOPK // (a + 1) for a in range(TOPK)]


def _oddeven_merge(lo, hi, r):
    step = r * 2
    if step < hi - lo:
        yield from _oddeven_merge(lo, hi, step)
        yield from _oddeven_merge(lo + r, hi, step)
        yield from [(i, i + r) for i in range(lo + r, hi - r, step)]
    else:
        yield (lo, lo + r)


def _oddeven_merge_sort(lo, hi):
    if hi - lo >= 1:
        mid = lo + (hi - lo) // 2
        yield from _oddeven_merge_sort(lo, mid)
        yield from _oddeven_merge_sort(mid + 1, hi)
        yield from _oddeven_merge(lo, hi, 1)


_NET16 = tuple(_oddeven_merge_sort(0, TOPK - 1))

ROUTE_TILES = SUBLANES // 2
ROUTE_TOKENS = ROUTE_TILES * LANES


def _beats(vb, ib, va, ia):
    if isinstance(ia, int) and isinstance(ib, int):
        return vb > va if ib > ia else vb >= va
    return (vb > va) | ((vb == va) & (ib < ia))


def _compare_exchange(v, i, a, b):
    swap = _beats(v[b], i[b], v[a], i[a])
    v[a], v[b] = jnp.where(swap, v[b], v[a]), jnp.where(swap, v[a], v[b])
    i[a], i[b] = jnp.where(swap, i[b], i[a]), jnp.where(swap, i[a], i[b])


def _merge_top16(va, ia, vb, ib):
    cv, ci = [], []
    for r in range(TOPK):
        o = TOPK - 1 - r
        take_b = _beats(vb[o], ib[o], va[r], ia[r])
        cv.append(jnp.where(take_b, vb[o], va[r]))
        ci.append(jnp.where(take_b, ib[o], ia[r]))
    dist = TOPK // 2
    while dist:
        for r in range(TOPK):
            if not r & dist:
                _compare_exchange(cv, ci, r, r + dist)
        dist //= 2
    return cv, ci


def _sorted_groups(load_key, which):
    groups = []
    for g in which:
        v = [load_key(g * TOPK + r) for r in range(TOPK)]
        i = [g * TOPK + r for r in range(TOPK)]
        for a, b in _NET16:
            _compare_exchange(v, i, a, b)
        groups.append((v, i))
    return groups


def _top16(groups):
    while len(groups) > 1:
        groups = [_merge_top16(*groups[n], *groups[n + 1]) for n in range(0, len(groups), 2)]
    return groups[0]


def _pair_top16(v1, i1, v2, i2):
    cells = [(a, b) for a in range(TOPK) for b in range(_NB[a])]
    cand = {(a, b): v1[a] + v2[b] for a, b in cells}
    picked = []
    for r in range(TOPK):
        live = [(a, b) for a, b in cells if (a + 1) * (b + 1) <= r + 1]
        best = cand[live[0]]
        flat = jnp.zeros(best.shape, jnp.int32)
        for a, b in live[1:]:
            take = cand[a, b] > best
            best = jnp.where(take, cand[a, b], best)
            flat = jnp.where(take, a * TOPK + b, flat)
        for a, b in live:
            cand[a, b] = jnp.where(flat == a * TOPK + b, -jnp.inf, cand[a, b])
        a_sel = jnp.right_shift(flat, 4)
        b_sel = jnp.bitwise_and(flat, TOPK - 1)
        row, col = i1[r], i2[r]
        for n in range(r - 1, -1, -1):
            row = jnp.where(a_sel == n, i1[n], row)
            col = jnp.where(b_sel == n, i2[n], col)
        picked.append((best, row, col))
    return picked


def _route_sort(sc_ref, which):
    return _sorted_groups(lambda k: sc_ref[0, 0, pl.ds(k, SUBLANES, stride=SC_PITCH), :], which)


def _route_select(v, i, hd, buf, row_scr, col_scr, gate_scr):
    v2 = [pltpu.roll(x, ROUTE_TILES, axis=0) for x in v]
    i2 = [pltpu.roll(x, ROUTE_TILES, axis=0) for x in i]
    picked = _pair_top16(v, i, v2, i2)
    e = [jnp.exp(sc - picked[0][0]) for sc, _, _ in picked]
    z = e[0]
    for ek in e[1:]:
        z = z + ek
    for n, (_, row, col) in enumerate(picked):
        base = pl.multiple_of((hd * TOPK + n) * SUBLANES, SUBLANES)
        row_scr[buf, pl.ds(base, SUBLANES), :] = row
        col_scr[buf, pl.ds(base, SUBLANES), :] = col
        gate_scr[buf, pl.ds(base, SUBLANES), :] = e[n] / z


def _peer_kernel(h2_ref, sc_ref, u_ref, v_ref, x1_ref, mod_ref, gf_ref, o_ref,
                 g_scr, acc_scr, w_scr, row_scr, col_scr, gate_scr, rows_t, gates_t):
    tb = h2_ref.shape[0]
    eb = u_ref.shape[0]
    half = tb // 2
    step = pl.program_id(0)
    last_step = pl.num_programs(0) - 1
    j = pl.program_id(1)
    slot = (step * pl.num_programs(1) + j) % 2

    @pl.when(step == 0)
    def _route_first_block():
        @pl.when(j == 0)
        def _clear():
            acc_scr[...] = jnp.zeros_like(acc_scr)
            w_scr[...] = jnp.zeros_like(w_scr)

        v, i = _top16(_route_sort(sc_ref, range(N_KEYS // TOPK)))
        _route_select(v, i, j, 0, row_scr, col_scr, gate_scr)

    @pl.when((step > 0) & (step < last_step))
    def _evaluate_and_route():
        buf = (step - 1) % 2

        @pl.when(j == 0)
        def _build_gates():
            for c in range(ROUTE_TILES):
                tok = slice(c * LANES, (c + 1) * LANES)
                rows_t[tok, :] = row_scr[buf, pl.ds(c, N_SEL, stride=SUBLANES), :].T
                gates_t[tok, :] = gate_scr[buf, pl.ds(c, N_SEL, stride=SUBLANES), :].T
            key_iota = lax.broadcasted_iota(jnp.int32, (N_KEYS, N_SEL), 0)
            lane_iota = lax.broadcasted_iota(jnp.int32, (N_SEL, N_KEYS), 1).astype(_BF)
            one, zero = jnp.ones((), _BF), jnp.zeros((), _BF)

            def gate_matrix(t, col_of_slot):
                r = rows_t[pl.ds(t, 1), :]
                g = gates_t[pl.ds(t, 1), :]
                a = jnp.where(key_iota == r, g, 0.0).astype(_BF)
                bmat = jnp.where(lane_iota == col_of_slot, one, zero)
                return _dot(a, bmat)

            def column_ids(c):
                return col_scr[buf, pl.ds(c, N_SEL, stride=SUBLANES), :].astype(_F32).astype(_BF)

            def tile_pair(c, carry):
                lo = column_ids(c)
                hi = column_ids(c + half // LANES)
                for t in range(LANES):
                    pp = c * LANES + t
                    base = pl.multiple_of(pp * G_PITCH, SUBLANES)
                    g_scr[pl.ds(base, N_KEYS), :] = pltpu.pack_elementwise(
                        [gate_matrix(pp, lo[:, t:t + 1]), gate_matrix(pp + half, hi[:, t:t + 1])],
                        packed_dtype=_BF)
                return carry

            lax.fori_loop(0, half // LANES, tile_pair, 0)

        groups = _route_sort(sc_ref, range(N_KEYS // TOPK))
        a = _dot_nt(h2_ref[...], u_ref[...])
        gparts = []
        for ri in range(eb // N_KEYS):
            words = g_scr[pl.ds(j * (eb // N_KEYS) + ri, half, stride=G_PITCH), :]
            gparts.append(jnp.concatenate(
                [pltpu.unpack_elementwise(words, index=k, packed_dtype=_BF, unpacked_dtype=_F32)
                 for k in range(2)], axis=0))
        w = (_gelu(a) * jnp.concatenate(gparts, axis=-1)).astype(_BF)
        top_v, top_i = _top16(groups)
        _route_select(top_v, top_i, j, step % 2, row_scr, col_scr, gate_scr)
        acc_scr[...] += _dot(w_scr[1 - slot], v_ref[...])
        w_scr[slot] = w

    @pl.when((step == last_step) & (j == 0))
    def _last_product():
        acc_scr[...] += _dot(w_scr[1 - slot], v_ref[...])

    @pl.when((step > 1) & (j == 0))
    def _finish():
        gate2 = mod_ref[0, 5:6, :]
        x2 = x1_ref[...] + gate2 * acc_scr[...]
        o_ref[...] = _rms(x2, gf_ref[...])
        acc_scr[...] = jnp.zeros_like(acc_scr)


def _peer(h2, scores, u_bf, v_bf, x1, mod3, gf, eb, s):
    t, d = h2.shape
    e = u_bf.shape[0]
    tb = ROUTE_TOKENS
    nb = t // tb
    per_b = s // tb
    ne = PEER_HEADS
    assert e // eb == ne, "one routing head is scheduled per expert block"
    block = lambda i: jnp.clip(i, 0, nb - 1)
    evaluated = lambda i: block(i - 1)
    routed = lambda i: block(i)
    finished = lambda i, j: block(jnp.where(j == 0, i - 2, i - 1))
    first = lambda i, j: (jnp.where(i == 0, 0, jnp.where(i == nb + 1, ne - 1, j)), 0)
    second = lambda i, j: (jnp.where((i == 0) | (i == nb + 1), ne - 1, (j + ne - 1) % ne), 0)
    return pl.pallas_call(
        _peer_kernel,
        grid=(nb + 2, ne),
        in_specs=[
            pl.BlockSpec((tb, d), lambda i, j: (evaluated(i), 0)),
            pl.BlockSpec((1, 1) + scores.shape[2:], lambda i, j: (routed(i), j, 0, 0)),
            pl.BlockSpec((eb, d), first),
            pl.BlockSpec((eb, d), second),
            pl.BlockSpec((tb, d), lambda i, j: (finished(i, j), 0)),
            pl.BlockSpec((1, 6, d), lambda i, j: (finished(i, j) // per_b, 0, 0)),
            pl.BlockSpec((1, d), lambda i, j: (0, 0)),
        ],
        out_specs=pl.BlockSpec((tb, d), lambda i, j: (finished(i, j), 0)),
        out_shape=jax.ShapeDtypeStruct((t, d), _F32),
        scratch_shapes=[
            pltpu.VMEM((tb // 2 * G_PITCH, N_KEYS), jnp.uint32),
            pltpu.VMEM((tb, d), _F32),
            pltpu.VMEM((2, tb, eb), _BF),
            pltpu.VMEM((2, N_SEL * SUBLANES, LANES), jnp.int32),
            pltpu.VMEM((2, N_SEL * SUBLANES, LANES), jnp.int32),
            pltpu.VMEM((2, N_SEL * SUBLANES, LANES), _F32),
            pltpu.VMEM((tb, N_SEL), jnp.int32),
            pltpu.VMEM((tb, N_SEL), _F32),
        ],
        compiler_params=_cparams(("arbitrary", "arbitrary")),
        name="peer",
    )(h2, scores, u_bf, v_bf, x1, mod3, gf)


def _pad_cols(w, width):
    return jnp.pad(w, ((0, 0), (0, width - w.shape[1])))


def _rot_cols(w):
    half = w.shape[1] // 2
    return jnp.concatenate([-w[:, half:], w[:, :half]], axis=1)


def kernel(x, c, positions, w_ada, b_ada, g_norm1, w_in, g_q_a, w_uq, g_kv_a, w_ukv, g_sg, w_sg, b_sg,
           g_attn_out, g_sg_out, w_o, g_norm2, w_peer_q, peer_keys, peer_u, peer_v, g_final):
    b, s, d = x.shape
    depth = w_ada.shape[0]
    assert depth == 1, "the final norm is fused into the last layer's expert kernel"
    n_mod = w_ada.shape[2] // d
    t = b * s

    inv_freq = 1.0 / (ROPE_THETA ** (jnp.arange(0, ROPE, 2, dtype=_F32) / ROPE))
    invf = _pad_cols(jnp.concatenate([inv_freq, inv_freq])[None, :], LANES)
    pos3 = positions[:, :, None]
    c_pad = jnp.pad(c, ((0, SUBLANES - b), (0, 0)))

    o1 = Q_RANK
    o2 = o1 + KV_RANK
    o3 = o2 + ROPE
    xs = x
    for l in range(depth):
        mod = _adaln(c_pad, w_ada[l], b_ada[l][None, :])[:b]
        mod3 = mod.reshape(b, n_mod, d)

        w_kr = w_in[l][:, o2:o3]
        win_ext = jnp.concatenate(
            [w_in[l][:, :o2], _pad_cols(w_kr, LANES), _pad_cols(_rot_cols(w_kr), LANES), w_in[l][:, o3:]],
            axis=1).astype(_BF)
        wq3 = w_uq[l].reshape(Q_RANK, HEADS, QK_DIM)
        wq_nope = wq3[:, :, :NOPE].reshape(Q_RANK, HEADS * NOPE)
        wq_rope = [wq3[:, hd, NOPE:] for hd in range(HEADS)]
        wuq_ext = jnp.concatenate(
            [wq_nope] + [_pad_cols(w, LANES) for w in wq_rope]
            + [_pad_cols(_rot_cols(w), LANES) for w in wq_rope], axis=1).astype(_BF)

        q, k, v, ysg = _inproj(
            xs, pos3, mod3, g_norm1[l][None, :], win_ext, g_q_a[l][None, :], wuq_ext,
            g_kv_a[l][None, :], w_ukv[l].astype(_BF), invf, g_sg[l][None, :], w_sg[l].astype(_BF),
            b_sg[l][:, :, None], g_sg_out[l][None, :], ts=INPROJ_ROWS)
        o, u_bf, v_bf = _attention(q, k, v, peer_u[l], peer_v[l], tq=ATTN_Q_ROWS, tk=ATTN_K_ROWS)

        keys = peer_keys[l].reshape(2 * PEER_HEADS, N_KEYS, HALF_DIM).astype(_BF)
        x1, h2, scores = _post(
            o.reshape(t, HEADS * VDIM), ysg.reshape(t, SG_WIDTH), xs.reshape(t, d), mod3,
            g_attn_out[l][None, :], w_o[l].astype(_BF), g_norm2[l][None, :], w_peer_q[l].astype(_BF),
            keys, tt=ROUTE_TOKENS, s=s)
        out = _peer(h2, scores, u_bf, v_bf, x1, mod3,
                    g_final[None, :], eb=peer_u.shape[1] // PEER_HEADS, s=s)
        xs = out.reshape(b, s, d)
    return xs
```

```python
import functools
import math

import jax
import jax.numpy as jnp
from jax import lax
from jax.experimental import pallas as pl
from jax.experimental.pallas import tpu as pltpu

LANES = 128
SUBLANES = 8
VMEM_LIMIT_BYTES = 60 * 1024 * 1024

EPS = 1e-6
ROPE_THETA = 10000.0
HEADS = 4
NOPE = 128
ROPE = 64
VDIM = 128
QK_DIM = NOPE + ROPE
QK_PAD = 2 * LANES
Q_RANK = 256
KV_RANK = 128
SG_HEADS = 4
SG_DIM = 128
SG_CHUNK = 128
SG_WIDTH = SG_HEADS * SG_DIM
PEER_HEADS = 8
N_KEYS = 128
TOPK = 16
HALF_DIM = 128
N_SEL = PEER_HEADS * TOPK

G_PITCH = N_KEYS + SUBLANES
SC_PITCH = N_KEYS + SUBLANES

ADALN_COLS = 1024
INPROJ_ROWS = 512
ATTN_Q_ROWS = 1024
ATTN_K_ROWS = 256
ATTN_UNROLL = 8

_BF = jnp.bfloat16
_F32 = jnp.float32


def _cparams(sem):
    return pltpu.CompilerParams(dimension_semantics=sem, vmem_limit_bytes=VMEM_LIMIT_BYTES)


def _rms(x, g):
    return x * lax.rsqrt(jnp.mean(x * x, axis=-1, keepdims=True) + EPS) * g


def _gelu(x):
    return 0.5 * x * (1.0 + lax.erf(x * (1.0 / math.sqrt(2.0))))


def _dot(a, b):
    return jnp.dot(a, b, preferred_element_type=_F32)


def _dot_nt(a, b):
    return lax.dot_general(a, b, (((1,), (1,)), ((), ())), preferred_element_type=_F32)


def _adaln_kernel(c_ref, w_ref, b_ref, o_ref):
    c = c_ref[...]
    c_act = (c * jax.nn.sigmoid(c)).astype(_BF)
    o_ref[...] = _dot(c_act, w_ref[...].astype(_BF)) + b_ref[...]


def _adaln(c_pad, w_ada, b_ada):
    rows, d = c_pad.shape
    n = w_ada.shape[1]
    tn = ADALN_COLS
    return pl.pallas_call(
        _adaln_kernel,
        grid=(n // tn,),
        in_specs=[
            pl.BlockSpec((rows, d), lambda j: (0, 0)),
            pl.BlockSpec((d, tn), lambda j: (0, j)),
            pl.BlockSpec((1, tn), lambda j: (0, j)),
        ],
        out_specs=pl.BlockSpec((rows, tn), lambda j: (0, j)),
        out_shape=jax.ShapeDtypeStruct((rows, n), _F32),
        compiler_params=_cparams(("arbitrary",)),
        name="adaln",
    )(c_pad, w_ada, b_ada)


def _inproj_kernel(x_ref, pos_ref, mod_ref, g1_ref, win_ref, gq_ref, wuq_ref, gkv_ref, wukv_ref,
                   invf_ref, gsg_ref, wsg_ref, bsg_ref, gsgo_ref,
                   q_ref, k_ref, v_ref, ysg_ref, *, scale):
    ts = x_ref.shape[1]
    x = x_ref[0]
    shift1 = mod_ref[0, 0:1, :]
    scale1 = mod_ref[0, 1:2, :]
    h = _rms(x, g1_ref[...]) * (1.0 + scale1) + shift1
    z = _dot(h.astype(_BF), win_ref[...])

    o_kv = Q_RANK
    o_kr = o_kv + KV_RANK
    o_krot = o_kr + LANES
    o_u = o_krot + LANES
    o_v = o_u + SG_WIDTH

    ang = pos_ref[0].astype(_F32) * invf_ref[...]
    cos2 = jnp.cos(ang)
    sin2 = jnp.sin(ang)

    cq = _rms(z[:, :Q_RANK], gq_ref[...]).astype(_BF)
    qall = _dot(cq, wuq_ref[...])
    for hd in range(HEADS):
        q_nope = qall[:, hd * NOPE:(hd + 1) * NOPE]
        q_r = qall[:, HEADS * NOPE + hd * LANES: HEADS * NOPE + (hd + 1) * LANES]
        q_rot = qall[:, HEADS * (NOPE + LANES) + hd * LANES: HEADS * (NOPE + LANES) + (hd + 1) * LANES]
        q_ref[0, hd, :, 0:LANES] = (q_nope * scale).astype(_BF)
        q_ref[0, hd, :, LANES:QK_PAD] = ((q_r * cos2 + q_rot * sin2) * scale).astype(_BF)

    ckv = _rms(z[:, o_kv:o_kr], gkv_ref[...]).astype(_BF)
    kv = _dot(ckv, wukv_ref[...])
    k_rope = (z[:, o_kr:o_krot] * cos2 + z[:, o_krot:o_u] * sin2).astype(_BF)
    for hd in range(HEADS):
        base = hd * (NOPE + VDIM)
        k_ref[0, hd, :, 0:LANES] = kv[:, base:base + NOPE].astype(_BF)
        k_ref[0, hd, :, LANES:QK_PAD] = k_rope
        v_ref[0, hd] = kv[:, base + NOPE:base + NOPE + VDIM].astype(_BF)

    u = _gelu(z[:, o_u:o_v])
    vg = _rms(_gelu(z[:, o_v:o_v + SG_WIDTH]), gsg_ref[...]).astype(_BF)
    for ch in range(ts // SG_CHUNK):
        r0 = ch * SG_CHUNK
        parts = []
        for hd in range(SG_HEADS):
            c0 = hd * SG_DIM
            mix = _dot(wsg_ref[hd], vg[r0:r0 + SG_CHUNK, c0:c0 + SG_DIM]) + bsg_ref[hd]
            parts.append(u[r0:r0 + SG_CHUNK, c0:c0 + SG_DIM] * mix)
        y = jnp.concatenate(parts, axis=-1)
        ysg_ref[0, r0:r0 + SG_CHUNK, :] = _rms(y, gsgo_ref[...]).astype(_BF)


def _inproj(x, pos3, mod3, g1, win_ext, gq, wuq_ext, gkv, wukv, invf, gsg, wsg, bsg3, gsgo, ts):
    b, s, d = x.shape
    nz = win_ext.shape[1]
    full = lambda *shape: pl.BlockSpec(shape, lambda bi, i: (0,) * len(shape))
    return pl.pallas_call(
        functools.partial(_inproj_kernel, scale=QK_DIM ** -0.5 * math.log2(math.e)),
        grid=(b, s // ts),
        in_specs=[
            pl.BlockSpec((1, ts, d), lambda bi, i: (bi, i, 0)),
            pl.BlockSpec((1, ts, 1), lambda bi, i: (bi, i, 0)),
            pl.BlockSpec((1, 6, d), lambda bi, i: (bi, 0, 0)),
            full(1, d),
            full(d, nz),
            full(1, Q_RANK),
            full(Q_RANK, wuq_ext.shape[1]),
            full(1, KV_RANK),
            full(KV_RANK, wukv.shape[1]),
            full(1, LANES),
            full(1, SG_WIDTH),
            full(SG_HEADS, SG_CHUNK, SG_CHUNK),
            full(SG_HEADS, SG_CHUNK, 1),
            full(1, SG_WIDTH),
        ],
        out_specs=[
            pl.BlockSpec((1, HEADS, ts, QK_PAD), lambda bi, i: (bi, 0, i, 0)),
            pl.BlockSpec((1, HEADS, ts, QK_PAD), lambda bi, i: (bi, 0, i, 0)),
            pl.BlockSpec((1, HEADS, ts, VDIM), lambda bi, i: (bi, 0, i, 0)),
            pl.BlockSpec((1, ts, SG_WIDTH), lambda bi, i: (bi, i, 0)),
        ],
        out_shape=[
            jax.ShapeDtypeStruct((b, HEADS, s, QK_PAD), _BF),
            jax.ShapeDtypeStruct((b, HEADS, s, QK_PAD), _BF),
            jax.ShapeDtypeStruct((b, HEADS, s, VDIM), _BF),
            jax.ShapeDtypeStruct((b, s, SG_WIDTH), _BF),
        ],
        compiler_params=_cparams(("arbitrary", "arbitrary")),
        name="inproj",
    )(x, pos3, mod3, g1, win_ext, gq, wuq_ext, gkv, wukv, invf, gsg, wsg, bsg3, gsgo)


def _attn_kernel(q_ref, k_ref, v_ref, uf_ref, vf_ref, o_ref, ubf_ref, vbf_ref, s_scr, *, unroll):
    nk, tq, tk = s_scr.shape
    q = q_ref[0, 0]
    ones = jnp.ones((tk, LANES), _BF)

    def score_chunk(j, mrun):
        sc = _dot_nt(q, k_ref[0, 0, pl.ds(pl.multiple_of(j * tk, tk), tk), :])
        s_scr[j] = sc
        for c in range(tk // LANES):
            mrun = jnp.maximum(mrun, sc[:, c * LANES:(c + 1) * LANES])
        return mrun

    mrun = lax.fori_loop(0, nk, score_chunk, jnp.full((tq, LANES), -jnp.inf, _F32), unroll=unroll)
    m = jnp.max(mrun, axis=-1, keepdims=True)

    def value_chunk(j, acc):
        p = jnp.exp2(s_scr[j] - m).astype(_BF)
        vc = v_ref[0, 0, pl.ds(pl.multiple_of(j * tk, tk), tk), :]
        return acc + _dot(p, jnp.concatenate([vc, ones], axis=-1))

    acc = lax.fori_loop(0, nk, value_chunk, jnp.zeros((tq, VDIM + LANES), _F32), unroll=unroll)
    o_ref[0] = (acc[:, :VDIM] / acc[:, VDIM:]).astype(_BF)
    ubf_ref[...] = uf_ref[...].T.astype(_BF)
    vbf_ref[...] = vf_ref[...].astype(_BF)


def _attention(q, k, v, u_f32, v_f32, tq, tk):
    b, h, s, _ = q.shape
    nq = s // tq
    e, d = u_f32.shape
    er = e // (b * h * nq)
    step = lambda bi, hi, i: ((bi * h + hi) * nq + i, 0)
    return pl.pallas_call(
        functools.partial(_attn_kernel, unroll=ATTN_UNROLL),
        grid=(b, h, nq),
        scratch_shapes=[pltpu.VMEM((s // tk, tq, tk), _F32)],
        in_specs=[
            pl.BlockSpec((1, 1, tq, QK_PAD), lambda bi, hi, i: (bi, hi, i, 0)),
            pl.BlockSpec((1, 1, s, QK_PAD), lambda bi, hi, i: (bi, hi, 0, 0)),
            pl.BlockSpec((1, 1, s, VDIM), lambda bi, hi, i: (bi, hi, 0, 0)),
            pl.BlockSpec((er, d), step),
            pl.BlockSpec((er, d), step),
        ],
        out_specs=[
            pl.BlockSpec((1, tq, VDIM), lambda bi, hi, i: (bi, i, hi)),
            pl.BlockSpec((d, er), lambda bi, hi, i: step(bi, hi, i)[::-1]),
            pl.BlockSpec((er, d), step),
        ],
        out_shape=[
            jax.ShapeDtypeStruct((b, s, h * VDIM), _BF),
            jax.ShapeDtypeStruct((d, e), _BF),
            jax.ShapeDtypeStruct((e, d), _BF),
        ],
        compiler_params=_cparams(("arbitrary", "arbitrary", "arbitrary")),
        name="attention",
    )(q, k, v, u_f32, v_f32)


def _post_kernel(o_ref, ysg_ref, x_ref, mod_ref, gao_ref, wo_ref, g2_ref, wq_ref, keys_ref,
                 x1_ref, h2_ref, sc_ref):
    gate1 = mod_ref[0, 2:3, :]
    shift2 = mod_ref[0, 3:4, :]
    scale2 = mod_ref[0, 4:5, :]
    width = o_ref.shape[1]
    yn = _rms(o_ref[...].astype(_F32), gao_ref[...]).astype(_BF)
    y = _dot(yn, wo_ref[0:width, :]) + _dot(ysg_ref[...], wo_ref[width:, :])
    x1 = x_ref[...] + gate1 * y
    x1_ref[...] = x1
    h2 = (_rms(x1, g2_ref[...]) * (1.0 + scale2) + shift2).astype(_BF)
    h2_ref[...] = h2
    qp = _dot(h2, wq_ref[...]).astype(_BF)
    pad = jnp.zeros((SC_PITCH - N_KEYS, LANES), _F32)
    tiles = qp.shape[0] // LANES
    for hh in range(2 * PEER_HEADS):
        sc = _dot_nt(keys_ref[hh], qp[:, hh * HALF_DIM:(hh + 1) * HALF_DIM])
        for c in range(tiles):
            r0 = ((hh % 2) * tiles + c) * SC_PITCH
            sc_ref[0, hh // 2, r0:r0 + N_KEYS, :] = sc[:, c * LANES:(c + 1) * LANES]
            sc_ref[0, hh // 2, r0 + N_KEYS:r0 + SC_PITCH, :] = pad


def _post(o, ysg, x2d, mod3, gao, wo, g2, wq, keys, tt, s):
    t, d = x2d.shape
    width = o.shape[1]
    per_b = s // tt
    full = lambda *shape: pl.BlockSpec(shape, lambda i: (0,) * len(shape))
    return pl.pallas_call(
        _post_kernel,
        grid=(t // tt,),
        in_specs=[
            pl.BlockSpec((tt, width), lambda i: (i, 0)),
            pl.BlockSpec((tt, width), lambda i: (i, 0)),
            pl.BlockSpec((tt, d), lambda i: (i, 0)),
            pl.BlockSpec((1, 6, d), lambda i: (i // per_b, 0, 0)),
            full(1, width),
            full(2 * width, d),
            full(1, d),
            full(d, wq.shape[1]),
            full(2 * PEER_HEADS, N_KEYS, HALF_DIM),
        ],
        out_specs=[
            pl.BlockSpec((tt, d), lambda i: (i, 0)),
            pl.BlockSpec((tt, d), lambda i: (i, 0)),
            pl.BlockSpec((1, PEER_HEADS, 2 * tt // LANES * SC_PITCH, LANES), lambda i: (i, 0, 0, 0)),
        ],
        out_shape=[
            jax.ShapeDtypeStruct((t, d), _F32),
            jax.ShapeDtypeStruct((t, d), _BF),
            jax.ShapeDtypeStruct((t // tt, PEER_HEADS, 2 * tt // LANES * SC_PITCH, LANES), _F32),
        ],
        compiler_params=_cparams(("arbitrary",)),
        name="post",
    )(o, ysg, x2d, mod3, gao, wo, g2, wq, keys)


_NB = [TOPK // (a + 1) for a in range(TOPK)]


def _oddeven_merge(lo, hi, r):
    step = r * 2
    if step < hi - lo:
        yield from _oddeven_merge(lo, hi, step)
        yield from _oddeven_merge(lo + r, hi, step)
        yield from [(i, i + r) for i in range(lo + r, hi - r, step)]
    else:
        yield (lo, lo + r)


def _oddeven_merge_sort(lo, hi):
    if hi - lo >= 1:
        mid = lo + (hi - lo) // 2
        yield from _oddeven_merge_sort(lo, mid)
        yield from _oddeven_merge_sort(mid + 1, hi)
        yield from _oddeven_merge(lo, hi, 1)


_NET16 = tuple(_oddeven_merge_sort(0, TOPK - 1))

ROUTE_TILES = SUBLANES // 2
ROUTE_TOKENS = ROUTE_TILES * LANES


def _beats(vb, ib, va, ia):
    if isinstance(ia, int) and isinstance(ib, int):
        return vb > va if ib > ia else vb >= va
    return (vb > va) | ((vb == va) & (ib < ia))


def _compare_exchange(v, i, a, b):
    swap = _beats(v[b], i[b], v[a], i[a])
    v[a], v[b] = jnp.where(swap, v[b], v[a]), jnp.where(swap, v[a], v[b])
    i[a], i[b] = jnp.where(swap, i[b], i[a]), jnp.where(swap, i[a], i[b])


def _merge_top16(va, ia, vb, ib):
    cv, ci = [], []
    for r in range(TOPK):
        o = TOPK - 1 - r
        take_b = _beats(vb[o], ib[o], va[r], ia[r])
        cv.append(jnp.where(take_b, vb[o], va[r]))
        ci.append(jnp.where(take_b, ib[o], ia[r]))
    dist = TOPK // 2
    while dist:
        for r in range(TOPK):
            if not r & dist:
                _compare_exchange(cv, ci, r, r + dist)
        dist //= 2
    return cv, ci


def _sorted_groups(load_key, which):
    groups = []
    for g in which:
        v = [load_key(g * TOPK + r) for r in range(TOPK)]
        i = [g * TOPK + r for r in range(TOPK)]
        for a, b in _NET16:
            _compare_exchange(v, i, a, b)
        groups.append((v, i))
    return groups


def _top16(groups):
    while len(groups) > 1:
        groups = [_merge_top16(*groups[n], *groups[n + 1]) for n in range(0, len(groups), 2)]
    return groups[0]


def _pair_top16(v1, i1, v2, i2):
    cells = [(a, b) for a in range(TOPK) for b in range(_NB[a])]
    cand = {(a, b): v1[a] + v2[b] for a, b in cells}
    picked = []
    for r in range(TOPK):
        live = [(a, b) for a, b in cells if (a + 1) * (b + 1) <= r + 1]
        best = cand[live[0]]
        flat = jnp.zeros(best.shape, jnp.int32)
        for a, b in live[1:]:
            take = cand[a, b] > best
            best = jnp.where(take, cand[a, b], best)
            flat = jnp.where(take, a * TOPK + b, flat)
        for a, b in live:
            cand[a, b] = jnp.where(flat == a * TOPK + b, -jnp.inf, cand[a, b])
        a_sel = jnp.right_shift(flat, 4)
        b_sel = jnp.bitwise_and(flat, TOPK - 1)
        row, col = i1[r], i2[r]
        for n in range(r - 1, -1, -1):
            row = jnp.where(a_sel == n, i1[n], row)
            col = jnp.where(b_sel == n, i2[n], col)
        picked.append((best, row, col))
    return picked


def _route_sort(sc_ref, which):
    return _sorted_groups(lambda k: sc_ref[0, 0, pl.ds(k, SUBLANES, stride=SC_PITCH), :], which)


def _route_select(v, i, hd, buf, row_scr, col_scr, gate_scr):
    v2 = [pltpu.roll(x, ROUTE_TILES, axis=0) for x in v]
    i2 = [pltpu.roll(x, ROUTE_TILES, axis=0) for x in i]
    picked = _pair_top16(v, i, v2, i2)
    e = [jnp.exp(sc - picked[0][0]) for sc, _, _ in picked]
    z = e[0]
    for ek in e[1:]:
        z = z + ek
    for n, (_, row, col) in enumerate(picked):
        base = pl.multiple_of((hd * TOPK + n) * SUBLANES, SUBLANES)
        row_scr[buf, pl.ds(base, SUBLANES), :] = row
        col_scr[buf, pl.ds(base, SUBLANES), :] = col
        gate_scr[buf, pl.ds(base, SUBLANES), :] = e[n] / z


def _peer_kernel(h2_ref, sc_ref, u_ref, v_ref, x1_ref, mod_ref, gf_ref, o_ref,
                 g_scr, acc_scr, row_scr, col_scr, gate_scr, rows_t, gates_t):
    tb = h2_ref.shape[0]
    eb = v_ref.shape[0]
    half = tb // 2
    step = pl.program_id(0)
    j = pl.program_id(1)

    @pl.when(step == 0)
    def _route_first_block():
        v, i = _top16(_route_sort(sc_ref, range(N_KEYS // TOPK)))
        _route_select(v, i, j, 0, row_scr, col_scr, gate_scr)

    @pl.when(step > 0)
    def _evaluate_and_route():
        buf = (step - 1) % 2

        @pl.when(j == 0)
        def _build_gates():
            acc_scr[...] = jnp.zeros_like(acc_scr)
            for c in range(ROUTE_TILES):
                tok = slice(c * LANES, (c + 1) * LANES)
                rows_t[tok, :] = row_scr[buf, pl.ds(c, N_SEL, stride=SUBLANES), :].T
                gates_t[tok, :] = gate_scr[buf, pl.ds(c, N_SEL, stride=SUBLANES), :].T
            key_iota = lax.broadcasted_iota(jnp.int32, (N_KEYS, N_SEL), 0)
            lane_iota = lax.broadcasted_iota(jnp.int32, (N_SEL, N_KEYS), 1).astype(_BF)
            one, zero = jnp.ones((), _BF), jnp.zeros((), _BF)

            def gate_matrix(t, col_of_slot):
                r = rows_t[pl.ds(t, 1), :]
                g = gates_t[pl.ds(t, 1), :]
                a = jnp.where(key_iota == r, g, 0.0).astype(_BF)
                bmat = jnp.where(lane_iota == col_of_slot, one, zero)
                return _dot(a, bmat)

            def column_ids(c):
                return col_scr[buf, pl.ds(c, N_SEL, stride=SUBLANES), :].astype(_F32).astype(_BF)

            def tile_pair(c, carry):
                lo = column_ids(c)
                hi = column_ids(c + half // LANES)
                for t in range(LANES):
                    pp = c * LANES + t
                    base = pl.multiple_of(pp * G_PITCH, SUBLANES)
                    g_scr[pl.ds(base, N_KEYS), :] = pltpu.pack_elementwise(
                        [gate_matrix(pp, lo[:, t:t + 1]), gate_matrix(pp + half, hi[:, t:t + 1])],
                        packed_dtype=_BF)
                return carry

            lax.fori_loop(0, half // LANES, tile_pair, 0)

        groups = _route_sort(sc_ref, range(N_KEYS // TOPK))
        a = _dot(h2_ref[...], u_ref[...])
        gparts = []
        for ri in range(eb // N_KEYS):
            words = g_scr[pl.ds(j * (eb // N_KEYS) + ri, half, stride=G_PITCH), :]
            gparts.append(jnp.concatenate(
                [pltpu.unpack_elementwise(words, index=k, packed_dtype=_BF, unpacked_dtype=_F32)
                 for k in range(2)], axis=0))
        w = (_gelu(a) * jnp.concatenate(gparts, axis=-1)).astype(_BF)
        top_v, top_i = _top16(groups)
        _route_select(top_v, top_i, j, step % 2, row_scr, col_scr, gate_scr)
        acc_scr[...] += _dot(w, v_ref[...])

        @pl.when(j == pl.num_programs(1) - 1)
        def _finish():
            gate2 = mod_ref[0, 5:6, :]
            x2 = x1_ref[...] + gate2 * acc_scr[...]
            o_ref[...] = _rms(x2, gf_ref[...])


def _peer(h2, scores, u_bf, v_bf, x1, mod3, gf, eb, s):
    t, d = h2.shape
    e = v_bf.shape[0]
    tb = ROUTE_TOKENS
    nb = t // tb
    per_b = s // tb
    assert e // eb == PEER_HEADS, "one routing head is scheduled per expert block"
    evaluated = lambda i: jnp.maximum(i - 1, 0)
    routed = lambda i: jnp.minimum(i, nb - 1)
    experts = lambda i, j: (jnp.where(i == 0, 0, j), 0)
    return pl.pallas_call(
        _peer_kernel,
        grid=(nb + 1, PEER_HEADS),
        in_specs=[
            pl.BlockSpec((tb, d), lambda i, j: (evaluated(i), 0)),
            pl.BlockSpec((1, 1) + scores.shape[2:], lambda i, j: (routed(i), j, 0, 0)),
            pl.BlockSpec((d, eb), lambda i, j: experts(i, j)[::-1]),
            pl.BlockSpec((eb, d), experts),
            pl.BlockSpec((tb, d), lambda i, j: (evaluated(i), 0)),
            pl.BlockSpec((1, 6, d), lambda i, j: (evaluated(i) // per_b, 0, 0)),
            pl.BlockSpec((1, d), lambda i, j: (0, 0)),
        ],
        out_specs=pl.BlockSpec((tb, d), lambda i, j: (evaluated(i), 0)),
        out_shape=jax.ShapeDtypeStruct((t, d), _F32),
        scratch_shapes=[
            pltpu.VMEM((tb // 2 * G_PITCH, N_KEYS), jnp.uint32),
            pltpu.VMEM((tb, d), _F32),
            pltpu.VMEM((2, N_SEL * SUBLANES, LANES), jnp.int32),
            pltpu.VMEM((2, N_SEL * SUBLANES, LANES), jnp.int32),
            pltpu.VMEM((2, N_SEL * SUBLANES, LANES), _F32),
            pltpu.VMEM((tb, N_SEL), jnp.int32),
            pltpu.VMEM((tb, N_SEL), _F32),
        ],
        compiler_params=_cparams(("arbitrary", "arbitrary")),
        name="peer",
    )(h2, scores, u_bf, v_bf, x1, mod3, gf)


def _pad_cols(w, width):
    return jnp.pad(w, ((0, 0), (0, width - w.shape[1])))


def _rot_cols(w):
    half = w.shape[1] // 2
    return jnp.concatenate([-w[:, half:], w[:, :half]], axis=1)


def kernel(x, c, positions, w_ada, b_ada, g_norm1, w_in, g_q_a, w_uq, g_kv_a, w_ukv, g_sg, w_sg, b_sg,
           g_attn_out, g_sg_out, w_o, g_norm2, w_peer_q, peer_keys, peer_u, peer_v, g_final):
    b, s, d = x.shape
    depth = w_ada.shape[0]
    assert depth == 1, "the final norm is fused into the last layer's expert kernel"
    n_mod = w_ada.shape[2] // d
    t = b * s

    inv_freq = 1.0 / (ROPE_THETA ** (jnp.arange(0, ROPE, 2, dtype=_F32) / ROPE))
    invf = _pad_cols(jnp.concatenate([inv_freq, inv_freq])[None, :], LANES)
    pos3 = positions[:, :, None]
    c_pad = jnp.pad(c, ((0, SUBLANES - b), (0, 0)))

    o1 = Q_RANK
    o2 = o1 + KV_RANK
    o3 = o2 + ROPE
    xs = x
    for l in range(depth):
        mod = _adaln(c_pad, w_ada[l], b_ada[l][None, :])[:b]
        mod3 = mod.reshape(b, n_mod, d)

        w_kr = w_in[l][:, o2:o3]
        win_ext = jnp.concatenate(
            [w_in[l][:, :o2], _pad_cols(w_kr, LANES), _pad_cols(_rot_cols(w_kr), LANES), w_in[l][:, o3:]],
            axis=1).astype(_BF)
        wq3 = w_uq[l].reshape(Q_RANK, HEADS, QK_DIM)
        wq_nope = wq3[:, :, :NOPE].reshape(Q_RANK, HEADS * NOPE)
        wq_rope = [wq3[:, hd, NOPE:] for hd in range(HEADS)]
        wuq_ext = jnp.concatenate(
            [wq_nope] + [_pad_cols(w, LANES) for w in wq_rope]
            + [_pad_cols(_rot_cols(w), LANES) for w in wq_rope], axis=1).astype(_BF)

        q, k, v, ysg = _inproj(
            xs, pos3, mod3, g_norm1[l][None, :], win_ext, g_q_a[l][None, :], wuq_ext,
            g_kv_a[l][None, :], w_ukv[l].astype(_BF), invf, g_sg[l][None, :], w_sg[l].astype(_BF),
            b_sg[l][:, :, None], g_sg_out[l][None, :], ts=INPROJ_ROWS)
        o, u_bf, v_bf = _attention(q, k, v, peer_u[l], peer_v[l], tq=ATTN_Q_ROWS, tk=ATTN_K_ROWS)

        keys = peer_keys[l].reshape(2 * PEER_HEADS, N_KEYS, HALF_DIM).astype(_BF)
        x1, h2, scores = _post(
            o.reshape(t, HEADS * VDIM), ysg.reshape(t, SG_WIDTH), xs.reshape(t, d), mod3,
            g_attn_out[l][None, :], w_o[l].astype(_BF), g_norm2[l][None, :], w_peer_q[l].astype(_BF),
            keys, tt=ROUTE_TOKENS, s=s)
        out = _peer(h2, scores, u_bf, v_bf, x1, mod3,
                    g_final[None, :], eb=peer_u.shape[1] // PEER_HEADS, s=s)
        xs = out.reshape(b, s, d)
    return xs
```
